```python
import math
import jax
import jax.numpy as jnp
from jax import lax
import numpy as np

D_MODEL = 1024
BATCH = 4
SEQ = 4096
DEPTH = 4

N_MIXERS = 3
N_ATTN = (DEPTH + 2) // 3
N_SSM = (DEPTH + 1) // 3
N_GMLP = DEPTH // 3

DA_HEADS = 8
DA_HEAD_DIM = D_MODEL // (2 * DA_HEADS)
DA_V_DIM = 2 * DA_HEAD_DIM
Q_BLOCK = 128
ROPE_THETA = 10000.0

S5_GROUP = 16
S5_GROUPS = D_MODEL // S5_GROUP
S5_STATE = 64
DT_MIN = 1e-3
DT_MAX = 1e-1

GM_FFN = 6 * D_MODEL
GM_HALF = GM_FFN // 2
GM_HEADS = 8
GM_HEAD_DIM = GM_HALF // GM_HEADS
GM_CHUNK = 128

MLP_HIDDEN = 4 * D_MODEL
EPS = 1e-6

kernel_name = "hybrid_diffattn_s5_gmlp_trunk"


def rms_norm(x, gain):
    x32 = x.astype(jnp.float32)
    y = x32 * lax.rsqrt(jnp.mean(jnp.square(x32), axis=-1, keepdims=True) + EPS)
    return (y * gain.astype(jnp.float32)).astype(x.dtype)


def rope_tables(positions):
    inv_freq = ROPE_THETA ** (-jnp.arange(0, DA_HEAD_DIM, 2, dtype=jnp.float32) / DA_HEAD_DIM)
    ang = positions.astype(jnp.float32)[..., None] * inv_freq
    return jnp.cos(ang), jnp.sin(ang)


def apply_rope(x, cos, sin):
    x32 = x.astype(jnp.float32)
    c = cos[:, :, None, None, :]
    s = sin[:, :, None, None, :]
    x1, x2 = jnp.split(x32, 2, axis=-1)
    return jnp.concatenate([x1 * c - x2 * s, x2 * c + x1 * s], axis=-1).astype(x.dtype)


def diff_attention(h, cos, sin, w_qkv, q_gain, k_gain, lam, sub_gain, w_o, lambda_init):
    B, L, _ = h.shape
    qkv = h @ w_qkv
    q, k, v = jnp.split(qkv, 3, axis=-1)
    q = q.reshape(B, L, DA_HEADS, 2, DA_HEAD_DIM)
    k = k.reshape(B, L, DA_HEADS, 2, DA_HEAD_DIM)
    v = v.reshape(B, L, DA_HEADS, DA_V_DIM)
    q = apply_rope(rms_norm(q, q_gain), cos, sin)
    k = apply_rope(rms_norm(k, k_gain), cos, sin)
    lam32 = lam.astype(jnp.float32)
    lam_full = (jnp.exp(jnp.sum(lam32[0] * lam32[1])) - jnp.exp(jnp.sum(lam32[2] * lam32[3]))
                + lambda_init)
    n_blocks = L // Q_BLOCK
    q_blocks = jnp.moveaxis(q.reshape(B, n_blocks, Q_BLOCK, DA_HEADS, 2, DA_HEAD_DIM), 1, 0)
    key_pos = jnp.arange(L)
    scale = DA_HEAD_DIM ** -0.5

    def block(args):
        qb, blk = args
        q_pos = blk * Q_BLOCK + jnp.arange(Q_BLOCK)
        s = jnp.einsum('bqhcd,bkhcd->bhcqk', qb, k,
                       preferred_element_type=jnp.float32) * scale
        s = jnp.where(key_pos[None, :] <= q_pos[:, None], s, -jnp.inf)
        p = jax.nn.softmax(s, axis=-1)
        a = p[:, :, 0] - lam_full * p[:, :, 1]
        return jnp.einsum('bhqk,bkhe->bqhe', a.astype(v.dtype), v)

    o = lax.map(block, (q_blocks, jnp.arange(n_blocks)))
    o = jnp.moveaxis(o, 0, 1).reshape(B, L, DA_HEADS, DA_V_DIM)
    o = rms_norm(o, sub_gain) * (1.0 - lambda_init)
    return o.reshape(B, L, D_MODEL) @ w_o


def _linear_recurrence(e_i, e_j):
    a_i, b_i = e_i
    a_j, b_j = e_j
    return a_j * a_i, a_j * b_i + b_j


def s5_mixer(h, w_in, a_re, a_im, b_re, b_im, c_re, c_im, d_skip, log_dt, w_glu, b_glu, w_out):
    B, L, _ = h.shape
    f32 = jnp.float32
    u = h @ w_in
    lam = lax.complex(jnp.minimum(a_re.astype(f32), -1e-4), a_im.astype(f32))
    dt = jnp.exp(log_dt.astype(f32))[:, None]
    lam_bar = jnp.exp(lam * dt)
    zoh = (lam_bar - 1.0) / lam
    b_cplx = lax.complex(b_re.astype(f32), b_im.astype(f32))
    b_bar = zoh[..., None] * b_cplx
    ug = u.astype(f32).reshape(B, L, S5_GROUPS, S5_GROUP)
    bu = lax.complex(jnp.einsum('blgh,gph->blgp', ug, jnp.real(b_bar)),
                     jnp.einsum('blgh,gph->blgp', ug, jnp.imag(b_bar)))
    a_elems = jnp.broadcast_to(lam_bar, (1, L, S5_GROUPS, S5_STATE))
    _, states = lax.associative_scan(_linear_recurrence, (a_elems, bu), axis=1)
    y = (jnp.einsum('ghp,blgp->blgh', c_re.astype(f32), jnp.real(states))
         - jnp.einsum('ghp,blgp->blgh', c_im.astype(f32), jnp.imag(states)))
    y = y.reshape(B, L, D_MODEL) + d_skip.astype(f32) * u.astype(f32)
    g = jax.nn.gelu(y.astype(h.dtype))
    g = g * jax.nn.sigmoid(g @ w_glu + b_glu)
    return g @ w_out


def gmlp_mixer(h, w_in, v_gain, w_s, b_s, w_out):
    B, L, _ = h.shape
    z = jax.nn.gelu(h @ w_in)
    u, v = jnp.split(z, 2, axis=-1)
    v = rms_norm(v, v_gain)
    n_chunks = L // GM_CHUNK
    v = v.reshape(B, n_chunks, GM_CHUNK, GM_HEADS, GM_HEAD_DIM)
    causal = jnp.tril(jnp.ones((GM_CHUNK, GM_CHUNK), dtype=bool))
    w_masked = jnp.where(causal[None], w_s, 0.0).astype(v.dtype)
    gate = (jnp.einsum('hts,bcshd->bcthd', w_masked, v)
            + b_s.T.astype(v.dtype)[None, None, :, :, None])
    gate = gate.reshape(B, L, GM_HALF)
    return (u * gate) @ w_out


def squared_relu_mlp(h, w1, w2):
    return jnp.square(jax.nn.relu(h @ w1)) @ w2


def setup_inputs(seed: int = 0) -> dict:
    key = jax.random.key(seed)
    ks = jax.random.split(key, 30)
    f32 = jnp.float32

    def dense(k, shape, fan_in):
        return jax.random.normal(k, shape, f32) * fan_in ** -0.5

    def gain(k, shape):
        return 1.0 + 0.02 * jax.random.normal(k, shape, f32)

    x = jax.random.normal(ks[0], (BATCH, SEQ, D_MODEL), f32)
    offset = jax.random.randint(ks[1], (BATCH, 1), 0, 2048, dtype=jnp.int32)
    positions = offset + jnp.arange(SEQ, dtype=jnp.int32)[None, :]
    n = jnp.arange(S5_STATE, dtype=f32)
    a_shape = (N_SSM, S5_GROUPS, S5_STATE)
    return {
        'x': x,
        'positions': positions,
        'norm_mix': gain(ks[2], (DEPTH, D_MODEL)),
        'norm_mlp': gain(ks[3], (DEPTH, D_MODEL)),
        'mlp_w1': dense(ks[4], (DEPTH, D_MODEL, MLP_HIDDEN), D_MODEL),
        'mlp_w2': dense(ks[5], (DEPTH, MLP_HIDDEN, D_MODEL), MLP_HIDDEN),
        'attn_w_qkv': dense(ks[6], (N_ATTN, D_MODEL, 3 * D_MODEL), D_MODEL),
        'attn_q_norm': gain(ks[7], (N_ATTN, DA_HEAD_DIM)),
        'attn_k_norm': gain(ks[8], (N_ATTN, DA_HEAD_DIM)),
        'attn_lambda': 0.1 * jax.random.normal(ks[9], (N_ATTN, 4, DA_HEAD_DIM), f32),
        'attn_sub_norm': gain(ks[10], (N_ATTN, DA_V_DIM)),
        'attn_w_o': dense(ks[11], (N_ATTN, D_MODEL, D_MODEL), D_MODEL),
        'ssm_w_in': dense(ks[12], (N_SSM, D_MODEL, D_MODEL), D_MODEL),
        'ssm_a_re': -0.5 + 0.01 * jax.random.normal(ks[13], a_shape, f32),
        'ssm_a_im': math.pi * n + 0.01 * jax.random.normal(ks[14], a_shape, f32),
        'ssm_b_re': dense(ks[15], (N_SSM, S5_GROUPS, S5_STATE, S5_GROUP), 2 * S5_GROUP),
        'ssm_b_im': dense(ks[16], (N_SSM, S5_GROUPS, S5_STATE, S5_GROUP), 2 * S5_GROUP),
        'ssm_c_re': 0.5 * jax.random.normal(ks[17], (N_SSM, S5_GROUPS, S5_GROUP, S5_STATE), f32),
        'ssm_c_im': 0.5 * jax.random.normal(ks[18], (N_SSM, S5_GROUPS, S5_GROUP, S5_STATE), f32),
        'ssm_d': jax.random.normal(ks[19], (N_SSM, D_MODEL), f32),
        'ssm_log_dt': jax.random.uniform(ks[20], (N_SSM, S5_GROUPS), f32,
                                         minval=math.log(DT_MIN), maxval=math.log(DT_MAX)),
        'ssm_w_glu': dense(ks[21], (N_SSM, D_MODEL, D_MODEL), D_MODEL),
        'ssm_b_glu': 0.02 * jax.random.normal(ks[22], (N_SSM, D_MODEL), f32),
        'ssm_w_out': dense(ks[23], (N_SSM, D_MODEL, D_MODEL), D_MODEL),
        'gm_w_in': dense(ks[24], (N_GMLP, D_MODEL, GM_FFN), D_MODEL),
        'gm_v_norm': gain(ks[25], (N_GMLP, GM_HALF)),
        'gm_w_s': dense(ks[26], (N_GMLP, GM_HEADS, GM_CHUNK, GM_CHUNK), GM_CHUNK),
        'gm_b_s': 1.0 + 0.1 * jax.random.normal(ks[27], (N_GMLP, GM_HEADS, GM_CHUNK), f32),
        'gm_w_out': dense(ks[28], (N_GMLP, GM_HALF, D_MODEL), GM_HALF),
    }


def reference(x, positions, norm_mix, norm_mlp, mlp_w1, mlp_w2,
              attn_w_qkv, attn_q_norm, attn_k_norm, attn_lambda, attn_sub_norm, attn_w_o,
              ssm_w_in, ssm_a_re, ssm_a_im, ssm_b_re, ssm_b_im, ssm_c_re, ssm_c_im,
              ssm_d, ssm_log_dt, ssm_w_glu, ssm_b_glu, ssm_w_out,
              gm_w_in, gm_v_norm, gm_w_s, gm_b_s, gm_w_out):
    cos, sin = rope_tables(positions)
    for i in range(DEPTH):
        kind = i % N_MIXERS
        j = i // N_MIXERS
        h = rms_norm(x, norm_mix[i])
        if kind == 0:
            lambda_init = 0.8 - 0.6 * math.exp(-0.3 * i)
            m = diff_attention(h, cos, sin, attn_w_qkv[j], attn_q_norm[j], attn_k_norm[j],
                               attn_lambda[j], attn_sub_norm[j], attn_w_o[j], lambda_init)
        elif kind == 1:
            m = s5_mixer(h, ssm_w_in[j], ssm_a_re[j], ssm_a_im[j], ssm_b_re[j], ssm_b_im[j],
                         ssm_c_re[j], ssm_c_im[j], ssm_d[j], ssm_log_dt[j],
                         ssm_w_glu[j], ssm_b_glu[j], ssm_w_out[j])
        else:
            m = gmlp_mixer(h, gm_w_in[j], gm_v_norm[j], gm_w_s[j], gm_b_s[j], gm_w_out[j])
        x = x + m
        x = x + squared_relu_mlp(rms_norm(x, norm_mlp[i]), mlp_w1[i], mlp_w2[i])
    return x
```

```python
import functools
import math

import jax
import jax.numpy as jnp
from jax import lax
from jax.experimental import pallas as pl
from jax.experimental.pallas import tpu as pltpu

F32 = jnp.float32
BF16 = jnp.bfloat16

EPS = 1e-6
ROPE_THETA = 10000.0
N_MIXERS = 3

DA_HEADS = 8
DA_HEAD_DIM = 64
S5_GROUP = 16
S5_STATE = 64
S5_CHUNK = 16
GM_HEADS = 8
GM_CHUNK = 128

LANES = 128
VMEM_LIMIT = 56 * 1024 * 1024
NEG_BIG = -1e30


def _params(*sem):
    return pltpu.CompilerParams(dimension_semantics=sem, vmem_limit_bytes=VMEM_LIMIT)


def _resident(shape):
    zeros = (0,) * len(shape)
    return pl.BlockSpec(shape, lambda *_: zeros, pipeline_mode=pl.Buffered(1))


def _rms(x, gain):
    return x * lax.rsqrt(jnp.mean(x * x, axis=-1, keepdims=True) + EPS) * gain


def _gelu(x):
    c = math.sqrt(2.0 / math.pi)
    return 0.5 * x * (1.0 + jnp.tanh(c * (x + 0.044715 * (x * x * x))))


def _dot(a, b):
    return jnp.dot(a, b, preferred_element_type=F32)


def _rope_body(pos_ref, freq_ref, cos_ref, sin_ref):
    ang = pos_ref[...].astype(F32) * freq_ref[...]
    lane = lax.broadcasted_iota(jnp.int32, ang.shape, 1)
    s = jnp.sin(ang)
    cos_ref[...] = jnp.cos(ang)
    sin_ref[...] = jnp.where(lane % DA_HEAD_DIM < DA_HEAD_DIM // 2, -s, s)


def _rope_tables(positions):
    n = positions.size
    tm = min(n, 2048)
    inv_freq = ROPE_THETA ** (-jnp.arange(0, DA_HEAD_DIM, 2, dtype=F32) / DA_HEAD_DIM)
    freq = jnp.tile(inv_freq, LANES // (DA_HEAD_DIM // 2))[None, :]
    return pl.pallas_call(
        _rope_body,
        out_shape=(jax.ShapeDtypeStruct((n, LANES), F32),) * 2,
        grid=(n // tm,),
        in_specs=[pl.BlockSpec((tm, 1), lambda i: (i, 0)), _resident((1, LANES))],
        out_specs=(pl.BlockSpec((tm, LANES), lambda i: (i, 0)),) * 2,
        compiler_params=_params("parallel"),
        name="rope_tables",
    )(positions.reshape(n, 1), freq)


def _qkv_body(x_ref, g_ref, w_ref, qg_ref, kg_ref, cos_ref, sin_ref, bd_ref,
              q_ref, k_ref, v_ref, *, d):
    h = _rms(x_ref[...], g_ref[...]).astype(BF16)
    cos = cos_ref[...]
    sin = sin_ref[...]
    lane = lax.broadcasted_iota(jnp.int32, cos.shape, 1)
    first_half = lane % DA_HEAD_DIM < DA_HEAD_DIM // 2
    bd = bd_ref[...]
    seg = bd.shape[0]

    def norm_rope(col0, gain_ref, out_ref, scale):
        acc = _dot(h, w_ref[:, col0:col0 + d])
        sq = (acc * acc).astype(BF16)
        gain = gain_ref[...]
        for c in range(d // seg):
            ss = _dot(sq[:, c * seg:(c + 1) * seg], bd)
            r = lax.rsqrt(ss * (1.0 / DA_HEAD_DIM) + EPS)
            for hh in range(seg // LANES):
                lo = c * seg + hh * LANES
                qn = acc[:, lo:lo + LANES] * r[:, hh * LANES:(hh + 1) * LANES] * gain
                partner = jnp.where(first_half,
                                    pltpu.roll(qn, LANES - DA_HEAD_DIM // 2, 1),
                                    pltpu.roll(qn, DA_HEAD_DIM // 2, 1))
                out = qn * cos + partner * sin
                out_ref[:, lo:lo + LANES] = (out * scale).astype(BF16)

    norm_rope(0, qg_ref, q_ref, DA_HEAD_DIM ** -0.5)
    norm_rope(d, kg_ref, k_ref, 1.0)
    v_ref[...] = _dot(h, w_ref[:, 2 * d:3 * d]).astype(BF16)


def _qkv_proj(x, gain, w_qkv, q_gain, k_gain, cos, sin, tm=512):
    n, d = x.shape
    tm = min(tm, n)
    seg = 2 * LANES
    r = jnp.arange(seg) // DA_HEAD_DIM
    bd = (r[:, None] == r[None, :]).astype(BF16)
    tile = lambda g: jnp.tile(g.astype(F32), LANES // DA_HEAD_DIM)[None, :]
    row = pl.BlockSpec((tm, d), lambda i: (i, 0))
    tab = pl.BlockSpec((tm, LANES), lambda i: (i, 0))
    return pl.pallas_call(
        functools.partial(_qkv_body, d=d),
        out_shape=(jax.ShapeDtypeStruct((n, d), BF16),) * 3,
        grid=(n // tm,),
        in_specs=[row, _resident((1, d)), _resident((d, 3 * d)), _resident((1, LANES)),
                  _resident((1, LANES)), tab, tab, _resident((seg, seg))],
        out_specs=(row,) * 3,
        compiler_params=_params("parallel"),
        name="qkv_proj",
    )(x, gain[None, :], w_qkv, tile(q_gain), tile(k_gain), cos, sin, bd)


def _flash_body(lam_ref, sg_ref, q_ref, k_ref, v_ref, o_ref, acc1_ref, acc2_ref,
                *, tq, lambda_init):
    i = pl.program_id(2)
    q = q_ref[...]
    lane = lax.broadcasted_iota(jnp.int32, q.shape, 1)
    zero = jnp.zeros_like(q)
    qs = (jnp.where(lane < DA_HEAD_DIM, q, zero), jnp.where(lane >= DA_HEAD_DIM, q, zero))
    accs = (acc1_ref, acc2_ref)
    acc1_ref[...] = jnp.zeros_like(acc1_ref)
    acc2_ref[...] = jnp.zeros_like(acc2_ref)

    def step(j, carry, masked):
        kb = k_ref[pl.ds(j * tq, tq), :]
        vb = v_ref[pl.ds(j * tq, tq), :]
        out = []
        for c in range(2):
            m, l = carry[2 * c], carry[2 * c + 1]
            s = lax.dot_general(qs[c], kb, (((1,), (1,)), ((), ())), preferred_element_type=F32)
            if masked:
                row = lax.broadcasted_iota(jnp.int32, s.shape, 0)
                col = lax.broadcasted_iota(jnp.int32, s.shape, 1)
                s = jnp.where(col <= row, s, NEG_BIG)
            m_new = jnp.maximum(m, jnp.max(s, axis=-1, keepdims=True))
            alpha = jnp.exp(m - m_new)
            p = jnp.exp(s - m_new)
            l = alpha * l + jnp.sum(p, axis=-1, keepdims=True)
            accs[c][...] = alpha * accs[c][...] + _dot(p.astype(BF16), vb)
            out += [m_new, l]
        return tuple(out)

    init = (jnp.full((tq, 1), NEG_BIG, F32), jnp.zeros((tq, 1), F32)) * 2
    carry = lax.fori_loop(0, i, lambda j, c: step(j, c, False), init)
    _, l1, _, l2 = step(i, carry, True)

    lam = lam_ref[...]
    dot_sum = lambda a, b: jnp.sum(lam[a:a + 1] * lam[b:b + 1], axis=-1, keepdims=True)
    lam_full = jnp.exp(dot_sum(0, 1)) - jnp.exp(dot_sum(2, 3)) + lambda_init
    o = acc1_ref[...] / l1 - lam_full * (acc2_ref[...] / l2)
    o_ref[...] = (_rms(o, sg_ref[...]) * (1.0 - lambda_init)).astype(BF16)


def _flash_attention(q, k, v, lam, sub_gain, lambda_init, tq=256):
    b, l, d = q.shape
    tq = min(tq, l)
    heads = d // LANES
    qspec = pl.BlockSpec((None, tq, LANES), lambda bi, hi, i: (bi, i, hi))
    kvspec = pl.BlockSpec((None, l, LANES), lambda bi, hi, i: (bi, 0, hi))
    return pl.pallas_call(
        functools.partial(_flash_body, tq=tq, lambda_init=lambda_init),
        out_shape=jax.ShapeDtypeStruct((b, l, d), BF16),
        grid=(b, heads, l // tq),
        in_specs=[_resident(lam.shape), _resident((1, LANES)), qspec, kvspec, kvspec],
        out_specs=qspec,
        scratch_shapes=[pltpu.VMEM((tq, LANES), F32), pltpu.VMEM((tq, LANES), F32)],
        compiler_params=_params("parallel", "parallel", "arbitrary"),
        name="diff_flash_attention",
    )(lam.astype(F32), sub_gain.astype(F32)[None, :], q, k, v)


def _proj_res_body(a_ref, w_ref, x_ref, o_ref):
    o_ref[...] = x_ref[...] + _dot(a_ref[...], w_ref[...])


def _proj_residual(a, w, x, tm=512):
    n, d = x.shape
    kdim = a.shape[1]
    tm = min(tm, n)
    row = pl.BlockSpec((tm, d), lambda i: (i, 0))
    return pl.pallas_call(
        _proj_res_body,
        out_shape=jax.ShapeDtypeStruct((n, d), F32),
        grid=(n // tm,),
        in_specs=[pl.BlockSpec((tm, kdim), lambda i: (i, 0)), _resident(w.shape), row],
        out_specs=row,
        compiler_params=_params("parallel"),
        name="proj_residual",
    )(a, w, x)


def _norm_proj_body(x_ref, g_ref, w_ref, o_ref):
    o_ref[...] = _dot(_rms(x_ref[...], g_ref[...]).astype(BF16), w_ref[...])


def _norm_proj(x, gain, w, tm=512):
    n, d = x.shape
    tm = min(tm, n)
    return pl.pallas_call(
        _norm_proj_body,
        out_shape=jax.ShapeDtypeStruct((n, w.shape[1]), F32),
        grid=(n // tm,),
        in_specs=[pl.BlockSpec((tm, d), lambda i: (i, 0)), _resident((1, d)), _resident(w.shape)],
        out_specs=pl.BlockSpec((tm, w.shape[1]), lambda i: (i, 0)),
        compiler_params=_params("parallel"),
        name="norm_proj",
    )(x, gain[None, :], w)


def _mlp_body(x_ref, g_ref, w1_ref, w2_ref, o_ref, *, th):
    x = x_ref[...]
    h = _rms(x, g_ref[...]).astype(BF16)
    acc = x
    for j in range(w1_ref.shape[1] // th):
        a = _dot(h, w1_ref[:, j * th:(j + 1) * th])
        a = jnp.square(jnp.maximum(a, 0.0)).astype(BF16)
        acc = acc + _dot(a, w2_ref[j * th:(j + 1) * th, :])
    o_ref[...] = acc


def _mlp(x, gain, w1, w2, tm=512, th=1024):
    n, d = x.shape
    tm = min(tm, n)
    row = pl.BlockSpec((tm, d), lambda i: (i, 0))
    return pl.pallas_call(
        functools.partial(_mlp_body, th=th),
        out_shape=jax.ShapeDtypeStruct((n, d), F32),
        grid=(n // tm,),
        in_specs=[row, _resident((1, d)), _resident(w1.shape), _resident(w2.shape)],
        out_specs=row,
        compiler_params=_params("parallel"),
        name="relu2_mlp",
    )(x, gain[None, :], w1, w2)


def _s5_scan_body(u_ref, m_ref, ws_ref, wo_ref, ar_ref, ai_ref, d_ref, o_ref,
                  u2_ref, s_ref, xp_ref, y_ref, *, n_chunks):
    t = S5_CHUNK
    half = ar_ref.shape[1]
    for s in range(t):
        u2_ref[:, s * LANES:(s + 1) * LANES] = u_ref[pl.ds(s, n_chunks, stride=t), :].astype(BF16)
    u2 = u2_ref[...]
    s_ref[...] = _dot(u2, ws_ref[...])
    ar = ar_ref[...]
    ai = ai_ref[...]

    def carry_step(c, carry):
        xr, xi = carry
        xp_ref[pl.ds(c, 1), 0:half] = xr
        xp_ref[pl.ds(c, 1), half:2 * half] = xi
        sr = s_ref[pl.ds(c, 1), 0:half]
        si = s_ref[pl.ds(c, 1), half:2 * half]
        return ar * xr - ai * xi + sr, ar * xi + ai * xr + si

    zero = jnp.zeros((1, half), F32)
    lax.fori_loop(0, n_chunks, carry_step, (zero, zero))
    y2 = _dot(u2, m_ref[...]) + _dot(xp_ref[...].astype(BF16), wo_ref[...])
    for s in range(t):
        y_ref[pl.ds(s, n_chunks, stride=t), :] = y2[:, s * LANES:(s + 1) * LANES]
    y = y_ref[...] + d_ref[...] * u_ref[...]
    o_ref[...] = _gelu(y).astype(BF16)


def _s5_weights(a_re, a_im, b_re, b_im, c_re, c_im, log_dt):
    t = S5_CHUNK
    g, p = a_re.shape
    gl = LANES // S5_GROUP
    nj = g // gl
    hp = lax.Precision.HIGHEST
    lam = lax.complex(jnp.minimum(a_re.astype(F32), -1e-4), a_im.astype(F32))
    dt = jnp.exp(log_dt.astype(F32))[:, None]
    lam_bar = jnp.exp(lam * dt)
    zoh = (lam_bar - 1.0) / lam
    b_bar = zoh[..., None] * lax.complex(b_re.astype(F32), b_im.astype(F32))
    c = lax.complex(c_re.astype(F32), c_im.astype(F32))
    steps = jnp.arange(t + 1, dtype=F32)
    pw = jnp.exp((lam * dt)[None] * steps[:, None, None])
    eye = jnp.eye(gl, dtype=F32)

    kern = jnp.real(jnp.einsum('ghp,kgp,gpi->kgih', c, pw[:t], b_bar, precision=hp))
    lag = jnp.arange(t)[None, :] - jnp.arange(t)[:, None]
    kst = jnp.where((lag >= 0)[:, :, None, None, None], kern[jnp.clip(lag, 0, t - 1)], 0.0)
    kst = kst.reshape(t, t, nj, gl, S5_GROUP, S5_GROUP)
    m = jnp.einsum('stjgih,gq->jsgitqh', kst, eye).reshape(nj, t * LANES, t * LANES)
    wst = pw[t - 1 - jnp.arange(t)][..., None] * b_bar[None]
    wst = jnp.stack([jnp.real(wst), jnp.imag(wst)], 0).reshape(2, t, nj, gl, p, S5_GROUP)
    ws = jnp.einsum('rsjgpi,gq->jsgirqp', wst, eye).reshape(nj, t * LANES, 2 * gl * p)
    cw = c[None] * pw[1:][:, :, None, :]
    cw = jnp.stack([jnp.real(cw), -jnp.imag(cw)], 0).reshape(2, t, nj, gl, S5_GROUP, p)
    wo = jnp.einsum('rtjghp,gq->jrgptqh', cw, eye).reshape(nj, 2 * gl * p, t * LANES)
    a_t = pw[t].reshape(nj, 1, gl * p)
    return (m.astype(BF16), ws.astype(BF16), wo.astype(BF16), jnp.real(a_t), jnp.imag(a_t))


def _s5_scan(u, weights, d_skip):
    b, l, d = u.shape
    m, ws, wo, ar, ai = weights
    nj = m.shape[0]
    n_chunks = l // S5_CHUNK
    tl = S5_CHUNK * LANES
    ns = ws.shape[2]
    col = pl.BlockSpec((None, l, LANES), lambda j, bi: (bi, 0, j))
    per_j = lambda shape: pl.BlockSpec((None,) + shape, lambda j, bi: (j, 0, 0),
                                       pipeline_mode=pl.Buffered(1))
    return pl.pallas_call(
        functools.partial(_s5_scan_body, n_chunks=n_chunks),
        out_shape=jax.ShapeDtypeStruct((b, l, d), BF16),
        grid=(nj, b),
        in_specs=[col, per_j((tl, tl)), per_j((tl, ns)), per_j((ns, tl)),
                  per_j((1, ns // 2)), per_j((1, ns // 2)),
                  pl.BlockSpec((1, LANES), lambda j, bi: (0, j))],
        out_specs=col,
        scratch_shapes=[pltpu.VMEM((n_chunks, tl), BF16), pltpu.VMEM((n_chunks, ns), F32),
                        pltpu.VMEM((n_chunks, ns), F32), pltpu.VMEM((l, LANES), F32)],
        compiler_params=_params("arbitrary", "arbitrary"),
        name="s5_scan",
    )(u, m, ws, wo, ar, ai, d_skip.astype(F32)[None, :])


def _s5_out_body(g_ref, wg_ref, bg_ref, wo_ref, x_ref, o_ref):
    g = g_ref[...]
    gate = jax.nn.sigmoid(_dot(g, wg_ref[...]) + bg_ref[...])
    o_ref[...] = x_ref[...] + _dot((g.astype(F32) * gate).astype(BF16), wo_ref[...])


def _s5_out(g, w_glu, b_glu, w_out, x, tm=512):
    n, d = x.shape
    tm = min(tm, n)
    row = pl.BlockSpec((tm, d), lambda i: (i, 0))
    return pl.pallas_call(
        _s5_out_body,
        out_shape=jax.ShapeDtypeStruct((n, d), F32),
        grid=(n // tm,),
        in_specs=[row, _resident(w_glu.shape), _resident((1, d)), _resident(w_out.shape), row],
        out_specs=row,
        compiler_params=_params("parallel"),
        name="s5_glu_out",
    )(g, w_glu, b_glu.astype(F32)[None, :], w_out, x)


def _gmlp_body(x_ref, g_ref, win_ref, vg_ref, wsp_ref, bs_ref, wout_ref, o_ref, v_ref,
               *, half, heads):
    x = x_ref[...]
    tm = x.shape[0]
    hd = half // heads
    h = _rms(x, g_ref[...]).astype(BF16)
    ss = jnp.zeros((tm, 1), F32)
    for c in range(heads):
        zv = _gelu(_dot(h, win_ref[:, half + c * hd:half + (c + 1) * hd]))
        ss = ss + jnp.sum(zv * zv, axis=-1, keepdims=True)
        v_ref[:, c * hd:(c + 1) * hd] = zv
    rinv = lax.rsqrt(ss * (1.0 / half) + EPS)
    row = lax.broadcasted_iota(jnp.int32, (GM_CHUNK, GM_CHUNK), 0)
    col = lax.broadcasted_iota(jnp.int32, (GM_CHUNK, GM_CHUNK), 1)
    acc = x
    for c in range(heads):
        sl = slice(c * hd, (c + 1) * hd)
        zu = _gelu(_dot(h, win_ref[:, sl]))
        vh = (v_ref[:, sl] * rinv * vg_ref[:, sl]).astype(BF16)
        wm = jnp.where(col <= row, wsp_ref[c], 0.0).astype(BF16)
        bias = bs_ref[:, c:c + 1]
        gate = jnp.concatenate(
            [_dot(wm, vh[r * GM_CHUNK:(r + 1) * GM_CHUNK, :]) + bias for r in range(tm // GM_CHUNK)],
            axis=0)
        acc = acc + _dot((zu * gate).astype(BF16), wout_ref[sl, :])
    o_ref[...] = acc


def _gmlp(x, gain, w_in, v_gain, w_s, b_s, w_out, tm=256):
    n, d = x.shape
    tm = min(tm, n)
    half = w_out.shape[0]
    heads = w_s.shape[0]
    row = pl.BlockSpec((tm, d), lambda i: (i, 0))
    return pl.pallas_call(
        functools.partial(_gmlp_body, half=half, heads=heads),
        out_shape=jax.ShapeDtypeStruct((n, d), F32),
        grid=(n // tm,),
        in_specs=[row, _resident((1, d)), _resident(w_in.shape), _resident((1, half)),
                  _resident(w_s.shape), _resident((GM_CHUNK, heads)), _resident(w_out.shape)],
        out_specs=row,
        scratch_shapes=[pltpu.VMEM((tm, half), F32)],
        compiler_params=_params("parallel"),
        name="gmlp",
    )(x, gain[None, :], w_in, v_gain.astype(F32)[None, :], w_s.astype(F32), b_s.astype(F32).T, w_out)


def kernel(x, positions, norm_mix, norm_mlp, mlp_w1, mlp_w2, attn_w_qkv, attn_q_norm, attn_k_norm, attn_lambda, attn_sub_norm, attn_w_o, ssm_w_in, ssm_a_re, ssm_a_im, ssm_b_re, ssm_b_im, ssm_c_re, ssm_c_im, ssm_d, ssm_log_dt, ssm_w_glu, ssm_b_glu, ssm_w_out, gm_w_in, gm_v_norm, gm_w_s, gm_b_s, gm_w_out):
    b, l, d = x.shape
    n = b * l
    depth = norm_mix.shape[0]
    bf = lambda w: w.astype(BF16)
    xf = x.reshape(n, d)
    cos, sin = _rope_tables(positions)
    for i in range(depth):
        kind = i % N_MIXERS
        j = i // N_MIXERS
        if kind == 0:
            lambda_init = 0.8 - 0.6 * math.exp(-0.3 * i)
            q, k, v = _qkv_proj(xf, norm_mix[i], bf(attn_w_qkv[j]), attn_q_norm[j], attn_k_norm[j],
                                cos, sin)
            shape = (b, l, d)
            o = _flash_attention(q.reshape(shape), k.reshape(shape), v.reshape(shape),
                                 attn_lambda[j], attn_sub_norm[j], lambda_init)
            xf = _proj_residual(o.reshape(n, d), bf(attn_w_o[j]), xf)
        elif kind == 1:
            u = _norm_proj(xf, norm_mix[i], bf(ssm_w_in[j]))
            weights = _s5_weights(ssm_a_re[j], ssm_a_im[j], ssm_b_re[j], ssm_b_im[j],
                                  ssm_c_re[j], ssm_c_im[j], ssm_log_dt[j])
            g = _s5_scan(u.reshape(b, l, d), weights, ssm_d[j])
            xf = _s5_out(g.reshape(n, d), bf(ssm_w_glu[j]), ssm_b_glu[j], bf(ssm_w_out[j]), xf)
        else:
            xf = _gmlp(xf, norm_mix[i], bf(gm_w_in[j]), gm_v_norm[j], gm_w_s[j], gm_b_s[j],
                       bf(gm_w_out[j]))
        xf = _mlp(xf, norm_mlp[i], bf(mlp_w1[i]), bf(mlp_w2[i]))
    return xf.reshape(b, l, d)
```

```python
import functools
import math

import jax
import jax.numpy as jnp
from jax import lax
from jax.experimental import pallas as pl
from jax.experimental.pallas import tpu as pltpu

F32 = jnp.float32
BF16 = jnp.bfloat16

EPS = 1e-6
ROPE_THETA = 10000.0
N_MIXERS = 3

DA_HEADS = 8
DA_HEAD_DIM = 64
S5_GROUP = 16
S5_STATE = 64
S5_CHUNK = 16
GM_HEADS = 8
GM_CHUNK = 128

LANES = 128
MXU_DIM = 256
VMEM_LIMIT = 56 * 1024 * 1024
NEG_BIG = -1e30
FAST_SCORE_BOUND = 30.0
LOG2E = math.log2(math.e)


def _params(*sem):
    return pltpu.CompilerParams(dimension_semantics=sem, vmem_limit_bytes=VMEM_LIMIT)


def _resident(shape):
    zeros = (0,) * len(shape)
    return pl.BlockSpec(shape, lambda *_: zeros, pipeline_mode=pl.Buffered(1))


def _rms(x, gain):
    return x * lax.rsqrt(jnp.mean(x * x, axis=-1, keepdims=True) + EPS) * gain


def _gelu(x):
    c = math.sqrt(2.0 / math.pi)
    return 0.5 * x * (1.0 + jnp.tanh(c * (x + 0.044715 * (x * x * x))))


def _dot(a, b):
    return jnp.dot(a, b, preferred_element_type=F32)


def _rope_body(pos_ref, freq_ref, cos_ref, sin_ref):
    ang = pos_ref[...].astype(F32) * freq_ref[...]
    lane = lax.broadcasted_iota(jnp.int32, ang.shape, 1)
    s = jnp.sin(ang)
    cos_ref[...] = jnp.cos(ang)
    sin_ref[...] = jnp.where(lane % DA_HEAD_DIM < DA_HEAD_DIM // 2, -s, s)


def _rope_tables(positions):
    n = positions.size
    tm = min(n, 2048)
    inv_freq = ROPE_THETA ** (-jnp.arange(0, DA_HEAD_DIM, 2, dtype=F32) / DA_HEAD_DIM)
    freq = jnp.tile(inv_freq, LANES // (DA_HEAD_DIM // 2))[None, :]
    return pl.pallas_call(
        _rope_body,
        out_shape=(jax.ShapeDtypeStruct((n, LANES), F32),) * 2,
        grid=(n // tm,),
        in_specs=[pl.BlockSpec((tm, 1), lambda i: (i, 0)), _resident((1, LANES))],
        out_specs=(pl.BlockSpec((tm, LANES), lambda i: (i, 0)),) * 2,
        compiler_params=_params("parallel"),
        name="rope_tables",
    )(positions.reshape(n, 1), freq)


def _qkv_body(x_ref, g_ref, w_ref, qg_ref, kg_ref, cos_ref, sin_ref, bd_ref,
              q_ref, k_ref, v_ref, *, d):
    h = _rms(x_ref[...], g_ref[...]).astype(BF16)
    cos = cos_ref[...]
    sin = sin_ref[...]
    lane = lax.broadcasted_iota(jnp.int32, cos.shape, 1)
    first_half = lane % DA_HEAD_DIM < DA_HEAD_DIM // 2
    bd = bd_ref[...]
    seg = bd.shape[0]

    def norm_rope(col0, gain_ref, out_ref, scale):
        acc = _dot(h, w_ref[:, col0:col0 + d])
        sq = (acc * acc).astype(BF16)
        gain = gain_ref[...]
        for c in range(d // seg):
            ss = _dot(sq[:, c * seg:(c + 1) * seg], bd)
            r = lax.rsqrt(ss * (1.0 / DA_HEAD_DIM) + EPS)
            for hh in range(seg // LANES):
                lo = c * seg + hh * LANES
                qn = acc[:, lo:lo + LANES] * r[:, hh * LANES:(hh + 1) * LANES] * gain
                partner = jnp.where(first_half,
                                    pltpu.roll(qn, LANES - DA_HEAD_DIM // 2, 1),
                                    pltpu.roll(qn, DA_HEAD_DIM // 2, 1))
                out = qn * cos + partner * sin
                out_ref[:, lo:lo + LANES] = (out * scale).astype(BF16)

    norm_rope(0, qg_ref, q_ref, DA_HEAD_DIM ** -0.5 * LOG2E)
    norm_rope(d, kg_ref, k_ref, 1.0)
    v_ref[...] = _dot(h, w_ref[:, 2 * d:3 * d]).astype(BF16)


def _qkv_proj(x, gain, w_qkv, q_gain, k_gain, cos, sin, tm=512):
    n, d = x.shape
    tm = min(tm, n)
    seg = MXU_DIM
    r = jnp.arange(seg) // DA_HEAD_DIM
    bd = (r[:, None] == r[None, :]).astype(BF16)
    tile = lambda g: jnp.tile(g.astype(F32), LANES // DA_HEAD_DIM)[None, :]
    row = pl.BlockSpec((tm, d), lambda i: (i, 0))
    tab = pl.BlockSpec((tm, LANES), lambda i: (i, 0))
    return pl.pallas_call(
        functools.partial(_qkv_body, d=d),
        out_shape=(jax.ShapeDtypeStruct((n, d), BF16),) * 3,
        grid=(n // tm,),
        in_specs=[row, _resident((1, d)), _resident((d, 3 * d)), _resident((1, LANES)),
                  _resident((1, LANES)), tab, tab, _resident((seg, seg))],
        out_specs=(row,) * 3,
        compiler_params=_params("parallel"),
        name="qkv_proj",
    )(x, gain[None, :], w_qkv, tile(q_gain), tile(k_gain), cos, sin, bd)


def _split_components(q):
    lane = lax.broadcasted_iota(jnp.int32, q.shape, 1)
    zero = jnp.zeros_like(q)
    return jnp.where(lane < DA_HEAD_DIM, q, zero), jnp.where(lane >= DA_HEAD_DIM, q, zero)


def _scores(qc, kb):
    return lax.dot_general(qc, kb, (((1,), (1,)), ((), ())), preferred_element_type=F32)


def _causal(shape):
    row = lax.broadcasted_iota(jnp.int32, shape, 0)
    col = lax.broadcasted_iota(jnp.int32, shape, 1)
    return col <= row


def _diff_combine(o1, o2, lam_ref, sg_ref, lambda_init):
    lam = lam_ref[...]
    dot_sum = lambda a, b: jnp.sum(lam[a:a + 1] * lam[b:b + 1], axis=-1, keepdims=True)
    lam_full = jnp.exp(dot_sum(0, 1)) - jnp.exp(dot_sum(2, 3)) + lambda_init
    o = o1 - lam_full * o2
    return (_rms(o, sg_ref[...]) * (1.0 - lambda_init)).astype(BF16)


def _flash_safe_body(lam_ref, sg_ref, q_ref, k_ref, v_ref, o_ref, acc1_ref, acc2_ref,
                     *, tq, lambda_init):
    i = pl.program_id(2)
    qs = _split_components(q_ref[...])
    accs = (acc1_ref, acc2_ref)
    acc1_ref[...] = jnp.zeros_like(acc1_ref)
    acc2_ref[...] = jnp.zeros_like(acc2_ref)

    def step(j, carry, masked):
        kb = k_ref[pl.ds(j * tq, tq), :]
        vb = v_ref[pl.ds(j * tq, tq), :]
        out = []
        for c in range(2):
            m, l = carry[2 * c], carry[2 * c + 1]
            s = _scores(qs[c], kb)
            if masked:
                s = jnp.where(_causal(s.shape), s, NEG_BIG)
            m_new = jnp.maximum(m, jnp.max(s, axis=-1, keepdims=True))
            alpha = jnp.exp2(m - m_new)
            p = jnp.exp2(s - m_new)
            l = alpha * l + jnp.sum(p, axis=-1, keepdims=True)
            accs[c][...] = alpha * accs[c][...] + _dot(p.astype(BF16), vb)
            out += [m_new, l]
        return tuple(out)

    init = (jnp.full((tq, 1), NEG_BIG, F32), jnp.zeros((tq, 1), F32)) * 2
    carry = lax.fori_loop(0, i, lambda j, c: step(j, c, False), init)
    _, l1, _, l2 = step(i, carry, True)
    o_ref[...] = _diff_combine(acc1_ref[...] / l1, acc2_ref[...] / l2, lam_ref, sg_ref, lambda_init)


def _flash_fast_body(lam_ref, sg_ref, q_ref, k_ref, v_ref, o_ref, acc_ref, p_ref,
                     *, tq, lambda_init):
    i = pl.program_id(2)
    q2 = jnp.concatenate(_split_components(q_ref[...]), axis=0)
    lane = lax.broadcasted_iota(jnp.int32, (tq, LANES), 1)
    ones_col = jnp.where(lane == 0, 1.0, 0.0).astype(BF16)
    row = lax.broadcasted_iota(jnp.int32, (2 * tq, tq), 0)
    col = lax.broadcasted_iota(jnp.int32, (2 * tq, tq), 1)
    delta = col - jnp.where(row >= tq, row - tq, row)

    def probs(j):
        p = jnp.exp2(_scores(q2, k_ref[pl.ds(j * tq, tq), :]))
        return jnp.where(delta <= (i - j) * tq, p, 0.0).astype(BF16)

    def weighted_values(p, j):
        vb = jnp.concatenate([v_ref[pl.ds(j * tq, tq), :], ones_col], axis=1)
        return _dot(p, vb)

    p_ref[...] = probs(0)
    acc_ref[...] = jnp.zeros_like(acc_ref)

    def body(j, carry):
        acc_ref[...] += weighted_values(p_ref[...], j - 1)
        p_ref[...] = probs(j)
        return carry

    lax.fori_loop(1, i + 1, body, 0)
    acc = acc_ref[...] + weighted_values(p_ref[...], i)
    o = acc[:, :LANES] / acc[:, LANES:LANES + 1]
    o_ref[...] = _diff_combine(o[:tq], o[tq:], lam_ref, sg_ref, lambda_init)


def _flash_call(body, name, scratch, q, k, v, lam, sub_gain, lambda_init, tq):
    b, l, d = q.shape
    heads = d // LANES
    qspec = pl.BlockSpec((None, tq, LANES), lambda bi, hi, i: (bi, i, hi))
    kvspec = pl.BlockSpec((None, l, LANES), lambda bi, hi, i: (bi, 0, hi))
    return pl.pallas_call(
        functools.partial(body, tq=tq, lambda_init=lambda_init),
        out_shape=jax.ShapeDtypeStruct((b, l, d), BF16),
        grid=(b, heads, l // tq),
        in_specs=[_resident(lam.shape), _resident((1, LANES)), qspec, kvspec, kvspec],
        out_specs=qspec,
        scratch_shapes=scratch,
        compiler_params=_params("parallel", "parallel", "arbitrary"),
        name=name,
    )(lam, sub_gain, q, k, v)


def _flash_attention(q, k, v, lam, sub_gain, score_bound, lambda_init, tq_fast=512, tq_safe=256):
    tqf = min(tq_fast, q.shape[1])
    tqs = min(tq_safe, q.shape[1])
    args = (q, k, v, lam.astype(F32), sub_gain.astype(F32)[None, :])
    fast_scratch = [pltpu.VMEM((2 * tqf, 2 * LANES), F32), pltpu.VMEM((2 * tqf, tqf), BF16)]
    safe_scratch = [pltpu.VMEM((tqs, LANES), F32)] * 2
    fast = lambda *a: _flash_call(_flash_fast_body, "flash_fast", fast_scratch, *a, lambda_init, tqf)
    safe = lambda *a: _flash_call(_flash_safe_body, "flash_safe", safe_scratch, *a, lambda_init, tqs)
    return lax.cond(score_bound <= FAST_SCORE_BOUND, fast, safe, *args)


def _proj_res_body(a_ref, w_ref, x_ref, o_ref):
    o_ref[...] = x_ref[...] + _dot(a_ref[...], w_ref[...])


def _proj_residual(a, w, x, tm=512):
    n, d = x.shape
    kdim = a.shape[1]
    tm = min(tm, n)
    row = pl.BlockSpec((tm, d), lambda i: (i, 0))
    return pl.pallas_call(
        _proj_res_body,
        out_shape=jax.ShapeDtypeStruct((n, d), F32),
        grid=(n // tm,),
        in_specs=[pl.BlockSpec((tm, kdim), lambda i: (i, 0)), _resident(w.shape), row],
        out_specs=row,
        compiler_params=_params("parallel"),
        name="proj_residual",
    )(a, w, x)


def _norm_proj_body(x_ref, g_ref, w_ref, o_ref):
    o_ref[...] = _dot(_rms(x_ref[...], g_ref[...]).astype(BF16), w_ref[...])


def _norm_proj(x, gain, w, tm=512):
    n, d = x.shape
    tm = min(tm, n)
    return pl.pallas_call(
        _norm_proj_body,
        out_shape=jax.ShapeDtypeStruct((n, w.shape[1]), F32),
        grid=(n // tm,),
        in_specs=[pl.BlockSpec((tm, d), lambda i: (i, 0)), _resident((1, d)), _resident(w.shape)],
        out_specs=pl.BlockSpec((tm, w.shape[1]), lambda i: (i, 0)),
        compiler_params=_params("parallel"),
        name="norm_proj",
    )(x, gain[None, :], w)


def _mlp_body(x_ref, g_ref, w1_ref, w2_ref, o_ref, *, th):
    x = x_ref[...]
    h = _rms(x, g_ref[...]).astype(BF16)
    acc = x
    for j in range(w1_ref.shape[1] // th):
        a = _dot(h, w1_ref[:, j * th:(j + 1) * th])
        a = jnp.square(jnp.maximum(a, 0.0)).astype(BF16)
        acc = acc + _dot(a, w2_ref[j * th:(j + 1) * th, :])
    o_ref[...] = acc


def _mlp(x, gain, w1, w2, tm=512, th=1024):
    n, d = x.shape
    tm = min(tm, n)
    row = pl.BlockSpec((tm, d), lambda i: (i, 0))
    return pl.pallas_call(
        functools.partial(_mlp_body, th=th),
        out_shape=jax.ShapeDtypeStruct((n, d), F32),
        grid=(n // tm,),
        in_specs=[row, _resident((1, d)), _resident(w1.shape), _resident(w2.shape)],
        out_specs=row,
        compiler_params=_params("parallel"),
        name="relu2_mlp",
    )(x, gain[None, :], w1, w2)


def _s5_build_operators(bd_ref, bb_ref, cb_ref, pw_ref, pwc_ref, m_ref, ws_ref, wo_ref):
    t = S5_CHUNK
    half = pw_ref.shape[2]
    m_ref[...] = jnp.zeros_like(m_ref)
    for s in range(t):
        for tt in range(s, t):
            m_ref[s * LANES:(s + 1) * LANES, tt * LANES:(tt + 1) * LANES] = bd_ref[tt - s]
    bbr, bbi = bb_ref[0], bb_ref[1]
    for s in range(t):
        pr = pw_ref[0, t - 1 - s:t - s, :]
        pi = pw_ref[1, t - 1 - s:t - s, :]
        ws_ref[s * LANES:(s + 1) * LANES, 0:half] = (bbr * pr - bbi * pi).astype(BF16)
        ws_ref[s * LANES:(s + 1) * LANES, half:2 * half] = (bbr * pi + bbi * pr).astype(BF16)
    cbr, cbi = cb_ref[0], cb_ref[1]
    for tt in range(t):
        pr = pwc_ref[0, :, tt + 1:tt + 2]
        pi = pwc_ref[1, :, tt + 1:tt + 2]
        wo_ref[0:half, tt * LANES:(tt + 1) * LANES] = (cbr * pr - cbi * pi).astype(BF16)
        wo_ref[half:2 * half, tt * LANES:(tt + 1) * LANES] = (-(cbr * pi + cbi * pr)).astype(BF16)


def _s5_scan_body(u_ref, bd_ref, bb_ref, cb_ref, pw_ref, pwc_ref, d_ref, o_ref,
                  m_ref, ws_ref, wo_ref, u2_ref, s_ref, xp_ref, y_ref, *, n_chunks):
    t = S5_CHUNK
    half = pw_ref.shape[2]

    @pl.when(pl.program_id(1) == 0)
    def _():
        _s5_build_operators(bd_ref, bb_ref, cb_ref, pw_ref, pwc_ref, m_ref, ws_ref, wo_ref)

    for s in range(t):
        u2_ref[:, s * LANES:(s + 1) * LANES] = u_ref[pl.ds(s, n_chunks, stride=t), :].astype(BF16)
    s_ref[...] = _dot(u2_ref[...], ws_ref[...])
    ar = pw_ref[0, t:t + 1, :]
    ai = pw_ref[1, t:t + 1, :]

    def carry_step(c, carry):
        xr, xi = carry
        xp_ref[pl.ds(c, 1), 0:half] = xr
        xp_ref[pl.ds(c, 1), half:2 * half] = xi
        sr = s_ref[pl.ds(c, 1), 0:half]
        si = s_ref[pl.ds(c, 1), half:2 * half]
        return ar * xr - ai * xi + sr, ar * xi + ai * xr + si

    zero = jnp.zeros((1, half), F32)
    lax.fori_loop(0, n_chunks, carry_step, (zero, zero))
    xp = xp_ref[...].astype(BF16)
    per = MXU_DIM // LANES
    for cb in range(t // per):
        cols = slice(cb * MXU_DIM, (cb + 1) * MXU_DIM)
        kdim = (cb + 1) * MXU_DIM
        y2 = _dot(u2_ref[:, 0:kdim], m_ref[0:kdim, cols]) + _dot(xp, wo_ref[:, cols])
        for r in range(per):
            y_ref[pl.ds(cb * per + r, n_chunks, stride=t), :] = y2[:, r * LANES:(r + 1) * LANES]
    y = y_ref[...] + d_ref[...] * u_ref[...]
    o_ref[...] = _gelu(y).astype(BF16)


def _s5_weights(a_re, a_im, b_re, b_im, c_re, c_im, log_dt):
    t = S5_CHUNK
    g, p = a_re.shape
    gl = LANES // S5_GROUP
    nj = g // gl
    hp = lax.Precision.HIGHEST
    lr = jnp.minimum(a_re.astype(F32), -1e-4)
    li = a_im.astype(F32)
    dt = jnp.exp(log_dt.astype(F32))[:, None]
    steps = jnp.arange(t + 1, dtype=F32)[:, None, None]
    mag = jnp.exp(steps * (lr * dt))
    pr = mag * jnp.cos(steps * (li * dt))
    pi = mag * jnp.sin(steps * (li * dt))
    nr, ni = pr[1] - 1.0, pi[1]
    den = lr * lr + li * li
    zr = ((nr * lr + ni * li) / den)[..., None]
    zi = ((ni * lr - nr * li) / den)[..., None]
    br, bi = b_re.astype(F32), b_im.astype(F32)
    bbr = zr * br - zi * bi
    bbi = zr * bi + zi * br
    cr, ci = c_re.astype(F32), c_im.astype(F32)
    wr = pr[:t, :, :, None] * bbr[None] - pi[:t, :, :, None] * bbi[None]
    wi = pr[:t, :, :, None] * bbi[None] + pi[:t, :, :, None] * bbr[None]
    kern = (jnp.einsum('ghp,kgpi->kgih', cr, wr, precision=hp)
            - jnp.einsum('ghp,kgpi->kgih', ci, wi, precision=hp))
    eye = jnp.eye(gl, dtype=F32)
    kern = kern.reshape(t, nj, gl, S5_GROUP, S5_GROUP)
    bd = jnp.einsum('kjgih,gq->jkgiqh', kern, eye).reshape(nj, t, LANES, LANES).astype(BF16)
    bb = jnp.stack([bbr, bbi]).reshape(2, nj, gl, p, S5_GROUP)
    bblk = jnp.einsum('rjgpi,gq->jrgiqp', bb, eye).reshape(nj, 2, LANES, gl * p)
    cc = jnp.stack([cr, ci]).reshape(2, nj, gl, S5_GROUP, p)
    cblk = jnp.einsum('rjghp,gq->jrgpqh', cc, eye).reshape(nj, 2, gl * p, LANES)
    pw = jnp.stack([pr, pi]).reshape(2, t + 1, nj, gl * p).transpose(2, 0, 1, 3)
    return bd, bblk, cblk, pw, pw.transpose(0, 1, 3, 2)


def _s5_scan(u, weights, d_skip):
    b, l, d = u.shape
    nj = weights[0].shape[0]
    n_chunks = l // S5_CHUNK
    tl = S5_CHUNK * LANES
    ns = 2 * weights[3].shape[3]
    col = pl.BlockSpec((None, l, LANES), lambda j, bi: (bi, 0, j))
    per_j = lambda w: pl.BlockSpec((None,) + w.shape[1:], lambda j, bi: (j,) + (0,) * (w.ndim - 1))
    return pl.pallas_call(
        functools.partial(_s5_scan_body, n_chunks=n_chunks),
        out_shape=jax.ShapeDtypeStruct((b, l, d), BF16),
        grid=(nj, b),
        in_specs=[col] + [per_j(w) for w in weights] + [pl.BlockSpec((1, LANES), lambda j, bi: (0, j))],
        out_specs=col,
        scratch_shapes=[pltpu.VMEM((tl, tl), BF16), pltpu.VMEM((tl, ns), BF16), pltpu.VMEM((ns, tl), BF16),
                        pltpu.VMEM((n_chunks, tl), BF16), pltpu.VMEM((n_chunks, ns), F32),
                        pltpu.VMEM((n_chunks, ns), F32), pltpu.VMEM((l, LANES), F32)],
        compiler_params=_params("arbitrary", "arbitrary"),
        name="s5_scan",
    )(u, *weights, d_skip.astype(F32)[None, :])


def _s5_out_body(g_ref, wg_ref, bg_ref, wo_ref, x_ref, o_ref):
    g = g_ref[...]
    gate = jax.nn.sigmoid(_dot(g, wg_ref[...]) + bg_ref[...])
    o_ref[...] = x_ref[...] + _dot((g.astype(F32) * gate).astype(BF16), wo_ref[...])


def _s5_out(g, w_glu, b_glu, w_out, x, tm=512):
    n, d = x.shape
    tm = min(tm, n)
    row = pl.BlockSpec((tm, d), lambda i: (i, 0))
    return pl.pallas_call(
        _s5_out_body,
        out_shape=jax.ShapeDtypeStruct((n, d), F32),
        grid=(n // tm,),
        in_specs=[row, _resident(w_glu.shape), _resident((1, d)), _resident(w_out.shape), row],
        out_specs=row,
        compiler_params=_params("parallel"),
        name="s5_glu_out",
    )(g, w_glu, b_glu.astype(F32)[None, :], w_out, x)


def _gmlp_body(x_ref, g_ref, win_ref, vg_ref, wsp_ref, bs_ref, wout_ref, o_ref, v_ref,
               *, half, heads):
    x = x_ref[...]
    tm = x.shape[0]
    hd = half // heads
    h = _rms(x, g_ref[...]).astype(BF16)
    ss = jnp.zeros((tm, 1), F32)
    for c in range(heads):
        zv = _gelu(_dot(h, win_ref[:, half + c * hd:half + (c + 1) * hd]))
        ss = ss + jnp.sum(zv * zv, axis=-1, keepdims=True)
        v_ref[:, c * hd:(c + 1) * hd] = zv
    rinv = lax.rsqrt(ss * (1.0 / half) + EPS)
    row = lax.broadcasted_iota(jnp.int32, (GM_CHUNK, GM_CHUNK), 0)
    col = lax.broadcasted_iota(jnp.int32, (GM_CHUNK, GM_CHUNK), 1)
    acc = x
    for c in range(heads):
        sl = slice(c * hd, (c + 1) * hd)
        zu = _gelu(_dot(h, win_ref[:, sl]))
        vh = (v_ref[:, sl] * rinv * vg_ref[:, sl]).astype(BF16)
        wm = jnp.where(col <= row, wsp_ref[c], 0.0).astype(BF16)
        bias = bs_ref[:, c:c + 1]
        gate = jnp.concatenate(
            [_dot(wm, vh[r * GM_CHUNK:(r + 1) * GM_CHUNK, :]) + bias for r in range(tm // GM_CHUNK)],
            axis=0)
        acc = acc + _dot((zu * gate).astype(BF16), wout_ref[sl, :])
    o_ref[...] = acc


def _gmlp(x, gain, w_in, v_gain, w_s, b_s, w_out, tm=256):
    n, d = x.shape
    tm = min(tm, n)
    half = w_out.shape[0]
    heads = w_s.shape[0]
    row = pl.BlockSpec((tm, d), lambda i: (i, 0))
    return pl.pallas_call(
        functools.partial(_gmlp_body, half=half, heads=heads),
        out_shape=jax.ShapeDtypeStruct((n, d), F32),
        grid=(n // tm,),
        in_specs=[row, _resident((1, d)), _resident(w_in.shape), _resident((1, half)),
                  _resident(w_s.shape), _resident((GM_CHUNK, heads)), _resident(w_out.shape)],
        out_specs=row,
        scratch_shapes=[pltpu.VMEM((tm, half), F32)],
        compiler_params=_params("parallel"),
        name="gmlp",
    )(x, gain[None, :], w_in, v_gain.astype(F32)[None, :], w_s.astype(F32), b_s.astype(F32).T, w_out)


def kernel(x, positions, norm_mix, norm_mlp, mlp_w1, mlp_w2, attn_w_qkv, attn_q_norm, attn_k_norm, attn_lambda, attn_sub_norm, attn_w_o, ssm_w_in, ssm_a_re, ssm_a_im, ssm_b_re, ssm_b_im, ssm_c_re, ssm_c_im, ssm_d, ssm_log_dt, ssm_w_glu, ssm_b_glu, ssm_w_out, gm_w_in, gm_v_norm, gm_w_s, gm_b_s, gm_w_out):
    b, l, d = x.shape
    n = b * l
    depth = norm_mix.shape[0]
    bf = lambda w: w.astype(BF16)
    xf = x.reshape(n, d)
    cos, sin = _rope_tables(positions)
    for i in range(depth):
        kind = i % N_MIXERS
        j = i // N_MIXERS
        if kind == 0:
            lambda_init = 0.8 - 0.6 * math.exp(-0.3 * i)
            q, k, v = _qkv_proj(xf, norm_mix[i], bf(attn_w_qkv[j]), attn_q_norm[j], attn_k_norm[j],
                                cos, sin)
            shape = (b, l, d)
            score_bound = (DA_HEAD_DIM ** 0.5 * jnp.max(jnp.abs(attn_q_norm[j]))
                           * jnp.max(jnp.abs(attn_k_norm[j])))
            o = _flash_attention(q.reshape(shape), k.reshape(shape), v.reshape(shape),
                                 attn_lambda[j], attn_sub_norm[j], score_bound, lambda_init)
            xf = _proj_residual(o.reshape(n, d), bf(attn_w_o[j]), xf)
        elif kind == 1:
            u = _norm_proj(xf, norm_mix[i], bf(ssm_w_in[j]))
            weights = _s5_weights(ssm_a_re[j], ssm_a_im[j], ssm_b_re[j], ssm_b_im[j],
                                  ssm_c_re[j], ssm_c_im[j], ssm_log_dt[j])
            g = _s5_scan(u.reshape(b, l, d), weights, ssm_d[j])
            xf = _s5_out(g.reshape(n, d), bf(ssm_w_glu[j]), ssm_b_glu[j], bf(ssm_w_out[j]), xf)
        else:
            xf = _gmlp(xf, norm_mix[i], bf(gm_w_in[j]), gm_v_norm[j], gm_w_s[j], gm_b_s[j],
                       bf(gm_w_out[j]))
        xf = _mlp(xf, norm_mlp[i], bf(mlp_w1[i]), bf(mlp_w2[i]))
    return xf.reshape(b, l, d)
```

```python
import functools
import math

import jax
import jax.numpy as jnp
from jax import lax
from jax.experimental import pallas as pl
from jax.experimental.pallas import tpu as pltpu

F32 = jnp.float32
BF16 = jnp.bfloat16

EPS = 1e-6
ROPE_THETA = 10000.0
N_MIXERS = 3

DA_HEADS = 8
DA_HEAD_DIM = 64
S5_GROUP = 16
S5_STATE = 64
S5_CHUNK = 16
GM_HEADS = 8
GM_CHUNK = 128

LANES = 128
MXU_DIM = 256
VMEM_LIMIT = 56 * 1024 * 1024
NEG_BIG = -1e30
FAST_SCORE_BOUND = 30.0
LOG2E = math.log2(math.e)


def _params(*sem):
    return pltpu.CompilerParams(dimension_semantics=sem, vmem_limit_bytes=VMEM_LIMIT)


def _resident(shape):
    zeros = (0,) * len(shape)
    return pl.BlockSpec(shape, lambda *_: zeros, pipeline_mode=pl.Buffered(1))


def _rms(x, gain):
    return x * lax.rsqrt(jnp.mean(x * x, axis=-1, keepdims=True) + EPS) * gain


def _gelu(x):
    c = math.sqrt(2.0 / math.pi)
    return 0.5 * x * (1.0 + jnp.tanh(c * (x + 0.044715 * (x * x * x))))


def _dot(a, b):
    return jnp.dot(a, b, preferred_element_type=F32)


def _rope_body(pos_ref, freq_ref, cos_ref, sin_ref):
    ang = pos_ref[...].astype(F32) * freq_ref[...]
    lane = lax.broadcasted_iota(jnp.int32, ang.shape, 1)
    s = jnp.sin(ang)
    cos_ref[...] = jnp.cos(ang)
    sin_ref[...] = jnp.where(lane % DA_HEAD_DIM < DA_HEAD_DIM // 2, -s, s)


def _rope_tables(positions):
    n = positions.size
    tm = min(n, 2048)
    inv_freq = ROPE_THETA ** (-jnp.arange(0, DA_HEAD_DIM, 2, dtype=F32) / DA_HEAD_DIM)
    freq = jnp.tile(inv_freq, LANES // (DA_HEAD_DIM // 2))[None, :]
    return pl.pallas_call(
        _rope_body,
        out_shape=(jax.ShapeDtypeStruct((n, LANES), F32),) * 2,
        grid=(n // tm,),
        in_specs=[pl.BlockSpec((tm, 1), lambda i: (i, 0)), _resident((1, LANES))],
        out_specs=(pl.BlockSpec((tm, LANES), lambda i: (i, 0)),) * 2,
        compiler_params=_params("parallel"),
        name="rope_tables",
    )(positions.reshape(n, 1), freq)


def _qkv_body(x_ref, g_ref, w_ref, qg_ref, kg_ref, cos_ref, sin_ref, bd_ref,
              q_ref, k_ref, v_ref, *, d):
    h = _rms(x_ref[...], g_ref[...]).astype(BF16)
    cos = cos_ref[...]
    sin = sin_ref[...]
    lane = lax.broadcasted_iota(jnp.int32, cos.shape, 1)
    first_half = lane % DA_HEAD_DIM < DA_HEAD_DIM // 2
    bd = bd_ref[...]
    seg = bd.shape[0]

    def norm_rope(col0, gain_ref, out_ref, scale):
        acc = _dot(h, w_ref[:, col0:col0 + d])
        sq = (acc * acc).astype(BF16)
        gain = gain_ref[...]
        for c in range(d // seg):
            ss = _dot(sq[:, c * seg:(c + 1) * seg], bd)
            r = lax.rsqrt(ss * (1.0 / DA_HEAD_DIM) + EPS)
            for hh in range(seg // LANES):
                lo = c * seg + hh * LANES
                qn = acc[:, lo:lo + LANES] * r[:, hh * LANES:(hh + 1) * LANES] * gain
                partner = jnp.where(first_half,
                                    pltpu.roll(qn, LANES - DA_HEAD_DIM // 2, 1),
                                    pltpu.roll(qn, DA_HEAD_DIM // 2, 1))
                out = qn * cos + partner * sin
                out_ref[:, lo:lo + LANES] = (out * scale).astype(BF16)

    norm_rope(0, qg_ref, q_ref, DA_HEAD_DIM ** -0.5 * LOG2E)
    norm_rope(d, kg_ref, k_ref, 1.0)
    v_ref[...] = _dot(h, w_ref[:, 2 * d:3 * d]).astype(BF16)


def _qkv_proj(x, gain, w_qkv, q_gain, k_gain, cos, sin, tm=512):
    n, d = x.shape
    tm = min(tm, n)
    seg = MXU_DIM
    r = jnp.arange(seg) // DA_HEAD_DIM
    bd = (r[:, None] == r[None, :]).astype(BF16)
    tile = lambda g: jnp.tile(g.astype(F32), LANES // DA_HEAD_DIM)[None, :]
    row = pl.BlockSpec((tm, d), lambda i: (i, 0))
    tab = pl.BlockSpec((tm, LANES), lambda i: (i, 0))
    return pl.pallas_call(
        functools.partial(_qkv_body, d=d),
        out_shape=(jax.ShapeDtypeStruct((n, d), BF16),) * 3,
        grid=(n // tm,),
        in_specs=[row, _resident((1, d)), _resident((d, 3 * d)), _resident((1, LANES)),
                  _resident((1, LANES)), tab, tab, _resident((seg, seg))],
        out_specs=(row,) * 3,
        compiler_params=_params("parallel"),
        name="qkv_proj",
    )(x, gain[None, :], w_qkv, tile(q_gain), tile(k_gain), cos, sin, bd)


def _split_components(q):
    lane = lax.broadcasted_iota(jnp.int32, q.shape, 1)
    zero = jnp.zeros_like(q)
    return jnp.where(lane < DA_HEAD_DIM, q, zero), jnp.where(lane >= DA_HEAD_DIM, q, zero)


def _scores(qc, kb):
    return lax.dot_general(qc, kb, (((1,), (1,)), ((), ())), preferred_element_type=F32)


def _causal(shape):
    row = lax.broadcasted_iota(jnp.int32, shape, 0)
    col = lax.broadcasted_iota(jnp.int32, shape, 1)
    return col <= row


def _diff_combine(o1, o2, lam_ref, sg_ref, lambda_init):
    lam = lam_ref[...]
    dot_sum = lambda a, b: jnp.sum(lam[a:a + 1] * lam[b:b + 1], axis=-1, keepdims=True)
    lam_full = jnp.exp(dot_sum(0, 1)) - jnp.exp(dot_sum(2, 3)) + lambda_init
    o = o1 - lam_full * o2
    return (_rms(o, sg_ref[...]) * (1.0 - lambda_init)).astype(BF16)


def _flash_safe_body(lam_ref, sg_ref, q_ref, k_ref, v_ref, o_ref, acc1_ref, acc2_ref,
                     *, tq, lambda_init):
    i = pl.program_id(2)
    qs = _split_components(q_ref[...])
    accs = (acc1_ref, acc2_ref)
    acc1_ref[...] = jnp.zeros_like(acc1_ref)
    acc2_ref[...] = jnp.zeros_like(acc2_ref)

    def step(j, carry, masked):
        kb = k_ref[pl.ds(j * tq, tq), :]
        vb = v_ref[pl.ds(j * tq, tq), :]
        out = []
        for c in range(2):
            m, l = carry[2 * c], carry[2 * c + 1]
            s = _scores(qs[c], kb)
            if masked:
                s = jnp.where(_causal(s.shape), s, NEG_BIG)
            m_new = jnp.maximum(m, jnp.max(s, axis=-1, keepdims=True))
            alpha = jnp.exp2(m - m_new)
            p = jnp.exp2(s - m_new)
            l = alpha * l + jnp.sum(p, axis=-1, keepdims=True)
            accs[c][...] = alpha * accs[c][...] + _dot(p.astype(BF16), vb)
            out += [m_new, l]
        return tuple(out)

    init = (jnp.full((tq, 1), NEG_BIG, F32), jnp.zeros((tq, 1), F32)) * 2
    carry = lax.fori_loop(0, i, lambda j, c: step(j, c, False), init)
    _, l1, _, l2 = step(i, carry, True)
    o_ref[...] = _diff_combine(acc1_ref[...] / l1, acc2_ref[...] / l2, lam_ref, sg_ref, lambda_init)


def _flash_fast_body(lam_ref, sg_ref, q_ref, k_ref, v_ref, o_ref, acc_ref, p_ref,
                     *, tq, heads, lambda_init):
    i = pl.program_id(2)
    lanes = lambda h: slice(h * LANES, (h + 1) * LANES)
    q2 = [jnp.concatenate(_split_components(q_ref[:, lanes(h)]), axis=0) for h in range(heads)]
    lane = lax.broadcasted_iota(jnp.int32, (tq, LANES), 1)
    ones_col = jnp.where(lane == 0, 1.0, 0.0).astype(BF16)
    row = lax.broadcasted_iota(jnp.int32, (2 * tq, tq), 0)
    col = lax.broadcasted_iota(jnp.int32, (2 * tq, tq), 1)
    delta = col - jnp.where(row >= tq, row - tq, row)

    def probs(h, j):
        p = jnp.exp2(_scores(q2[h], k_ref[pl.ds(j * tq, tq), lanes(h)]))
        return jnp.where(delta <= (i - j) * tq, p, 0.0).astype(BF16)

    def weighted_values(h, j):
        vb = jnp.concatenate([v_ref[pl.ds(j * tq, tq), lanes(h)], ones_col], axis=1)
        return _dot(p_ref[h], vb)

    for h in range(heads):
        p_ref[h] = probs(h, 0)
    acc_ref[...] = jnp.zeros_like(acc_ref)

    def body(j, carry):
        for h in range(heads):
            acc_ref[h] += weighted_values(h, j - 1)
            p_ref[h] = probs(h, j)
        return carry

    lax.fori_loop(1, i + 1, body, 0)
    for h in range(heads):
        acc = acc_ref[h] + weighted_values(h, i)
        o = acc[:, :LANES] / acc[:, LANES:LANES + 1]
        o_ref[:, lanes(h)] = _diff_combine(o[:tq], o[tq:], lam_ref, sg_ref, lambda_init)


def _flash_call(body, name, scratch, heads_per_step, q, k, v, lam, sub_gain, lambda_init, tq):
    b, l, d = q.shape
    width = heads_per_step * LANES
    qspec = pl.BlockSpec((None, tq, width), lambda bi, hi, i: (bi, i, hi))
    kvspec = pl.BlockSpec((None, l, width), lambda bi, hi, i: (bi, 0, hi))
    return pl.pallas_call(
        functools.partial(body, tq=tq, lambda_init=lambda_init),
        out_shape=jax.ShapeDtypeStruct((b, l, d), BF16),
        grid=(b, d // width, l // tq),
        in_specs=[_resident(lam.shape), _resident((1, LANES)), qspec, kvspec, kvspec],
        out_specs=qspec,
        scratch_shapes=scratch,
        compiler_params=_params("parallel", "parallel", "arbitrary"),
        name=name,
    )(lam, sub_gain, q, k, v)


def _flash_attention(q, k, v, lam, sub_gain, score_bound, lambda_init, tq_fast=512, tq_safe=256, hps=4):
    tqf = min(tq_fast, q.shape[1])
    tqs = min(tq_safe, q.shape[1])
    args = (q, k, v, lam.astype(F32), sub_gain.astype(F32)[None, :])
    fast_scratch = [pltpu.VMEM((hps, 2 * tqf, 2 * LANES), F32), pltpu.VMEM((hps, 2 * tqf, tqf), BF16)]
    safe_scratch = [pltpu.VMEM((tqs, LANES), F32)] * 2
    fast = lambda *a: _flash_call(functools.partial(_flash_fast_body, heads=hps), "flash_fast", fast_scratch,
                                  hps, *a, lambda_init, tqf)
    safe = lambda *a: _flash_call(_flash_safe_body, "flash_safe", safe_scratch, 1, *a, lambda_init, tqs)
    return lax.cond(score_bound <= FAST_SCORE_BOUND, fast, safe, *args)


def _proj_res_body(a_ref, w_ref, x_ref, o_ref):
    o_ref[...] = x_ref[...] + _dot(a_ref[...], w_ref[...])


def _proj_residual(a, w, x, tm=512):
    n, d = x.shape
    kdim = a.shape[1]
    tm = min(tm, n)
    row = pl.BlockSpec((tm, d), lambda i: (i, 0))
    return pl.pallas_call(
        _proj_res_body,
        out_shape=jax.ShapeDtypeStruct((n, d), F32),
        grid=(n // tm,),
        in_specs=[pl.BlockSpec((tm, kdim), lambda i: (i, 0)), _resident(w.shape), row],
        out_specs=row,
        compiler_params=_params("parallel"),
        name="proj_residual",
    )(a, w, x)


def _norm_proj_body(x_ref, g_ref, w_ref, o_ref):
    o_ref[...] = _dot(_rms(x_ref[...], g_ref[...]).astype(BF16), w_ref[...])


def _norm_proj(x, gain, w, tm=512):
    n, d = x.shape
    tm = min(tm, n)
    return pl.pallas_call(
        _norm_proj_body,
        out_shape=jax.ShapeDtypeStruct((n, w.shape[1]), F32),
        grid=(n // tm,),
        in_specs=[pl.BlockSpec((tm, d), lambda i: (i, 0)), _resident((1, d)), _resident(w.shape)],
        out_specs=pl.BlockSpec((tm, w.shape[1]), lambda i: (i, 0)),
        compiler_params=_params("parallel"),
        name="norm_proj",
    )(x, gain[None, :], w)


def _mlp_body(x_ref, g_ref, w1_ref, w2_ref, o_ref, *, th):
    x = x_ref[...]
    h = _rms(x, g_ref[...]).astype(BF16)
    acc = x
    for j in range(w1_ref.shape[1] // th):
        a = _dot(h, w1_ref[:, j * th:(j + 1) * th])
        a = jnp.square(jnp.maximum(a, 0.0)).astype(BF16)
        acc = acc + _dot(a, w2_ref[j * th:(j + 1) * th, :])
    o_ref[...] = acc


def _mlp(x, gain, w1, w2, tm=512, th=1024):
    n, d = x.shape
    tm = min(tm, n)
    row = pl.BlockSpec((tm, d), lambda i: (i, 0))
    return pl.pallas_call(
        functools.partial(_mlp_body, th=th),
        out_shape=jax.ShapeDtypeStruct((n, d), F32),
        grid=(n // tm,),
        in_specs=[row, _resident((1, d)), _resident(w1.shape), _resident(w2.shape)],
        out_specs=row,
        compiler_params=_params("parallel"),
        name="relu2_mlp",
    )(x, gain[None, :], w1, w2)


def _s5_build_operators(bd_ref, bb_ref, cb_ref, pw_ref, pwc_ref, m_ref, ws_ref, wo_ref):
    t = S5_CHUNK
    half = pw_ref.shape[2]
    m_ref[...] = jnp.zeros_like(m_ref)
    for s in range(t):
        for tt in range(s, t):
            m_ref[s * LANES:(s + 1) * LANES, tt * LANES:(tt + 1) * LANES] = bd_ref[tt - s]
    bbr, bbi = bb_ref[0], bb_ref[1]
    for s in range(t):
        pr = pw_ref[0, t - 1 - s:t - s, :]
        pi = pw_ref[1, t - 1 - s:t - s, :]
        ws_ref[s * LANES:(s + 1) * LANES, 0:half] = (bbr * pr - bbi * pi).astype(BF16)
        ws_ref[s * LANES:(s + 1) * LANES, half:2 * half] = (bbr * pi + bbi * pr).astype(BF16)
    cbr, cbi = cb_ref[0], cb_ref[1]
    for tt in range(t):
        pr = pwc_ref[0, :, tt + 1:tt + 2]
        pi = pwc_ref[1, :, tt + 1:tt + 2]
        wo_ref[0:half, tt * LANES:(tt + 1) * LANES] = (cbr * pr - cbi * pi).astype(BF16)
        wo_ref[half:2 * half, tt * LANES:(tt + 1) * LANES] = (-(cbr * pi + cbi * pr)).astype(BF16)


def _s5_scan_body(u_ref, bd_ref, bb_ref, cb_ref, pw_ref, pwc_ref, d_ref, o_ref,
                  m_ref, ws_ref, wo_ref, u2_ref, s_ref, xp_ref, y_ref, *, n_chunks):
    t = S5_CHUNK
    half = pw_ref.shape[2]

    @pl.when(pl.program_id(1) == 0)
    def _():
        _s5_build_operators(bd_ref, bb_ref, cb_ref, pw_ref, pwc_ref, m_ref, ws_ref, wo_ref)

    for s in range(t):
        u2_ref[:, s * LANES:(s + 1) * LANES] = u_ref[pl.ds(s, n_chunks, stride=t), :].astype(BF16)
    s_ref[...] = _dot(u2_ref[...], ws_ref[...])
    ar = pw_ref[0, t:t + 1, :]
    ai = pw_ref[1, t:t + 1, :]

    def carry_step(c, carry):
        xr, xi = carry
        xp_ref[pl.ds(c, 1), 0:half] = xr
        xp_ref[pl.ds(c, 1), half:2 * half] = xi
        sr = s_ref[pl.ds(c, 1), 0:half]
        si = s_ref[pl.ds(c, 1), half:2 * half]
        return ar * xr - ai * xi + sr, ar * xi + ai * xr + si

    zero = jnp.zeros((1, half), F32)
    lax.fori_loop(0, n_chunks, carry_step, (zero, zero))
    xp = xp_ref[...].astype(BF16)
    per = MXU_DIM // LANES
    for cb in range(t // per):
        cols = slice(cb * MXU_DIM, (cb + 1) * MXU_DIM)
        kdim = (cb + 1) * MXU_DIM
        y2 = _dot(u2_ref[:, 0:kdim], m_ref[0:kdim, cols]) + _dot(xp, wo_ref[:, cols])
        for r in range(per):
            y_ref[pl.ds(cb * per + r, n_chunks, stride=t), :] = y2[:, r * LANES:(r + 1) * LANES]
    y = y_ref[...] + d_ref[...] * u_ref[...]
    o_ref[...] = _gelu(y).astype(BF16)


def _s5_weights(a_re, a_im, b_re, b_im, c_re, c_im, log_dt):
    t = S5_CHUNK
    g, p = a_re.shape
    gl = LANES // S5_GROUP
    nj = g // gl
    hp = lax.Precision.HIGHEST
    lr = jnp.minimum(a_re.astype(F32), -1e-4)
    li = a_im.astype(F32)
    dt = jnp.exp(log_dt.astype(F32))[:, None]
    steps = jnp.arange(t + 1, dtype=F32)[:, None, None]
    mag = jnp.exp(steps * (lr * dt))
    pr = mag * jnp.cos(steps * (li * dt))
    pi = mag * jnp.sin(steps * (li * dt))
    nr, ni = pr[1] - 1.0, pi[1]
    den = lr * lr + li * li
    zr = ((nr * lr + ni * li) / den)[..., None]
    zi = ((ni * lr - nr * li) / den)[..., None]
    br, bi = b_re.astype(F32), b_im.astype(F32)
    bbr = zr * br - zi * bi
    bbi = zr * bi + zi * br
    cr, ci = c_re.astype(F32), c_im.astype(F32)
    wr = pr[:t, :, :, None] * bbr[None] - pi[:t, :, :, None] * bbi[None]
    wi = pr[:t, :, :, None] * bbi[None] + pi[:t, :, :, None] * bbr[None]
    kern = (jnp.einsum('ghp,kgpi->kgih', cr, wr, precision=hp)
            - jnp.einsum('ghp,kgpi->kgih', ci, wi, precision=hp))
    eye = jnp.eye(gl, dtype=F32)
    kern = kern.reshape(t, nj, gl, S5_GROUP, S5_GROUP)
    bd = jnp.einsum('kjgih,gq->jkgiqh', kern, eye).reshape(nj, t, LANES, LANES).astype(BF16)
    bb = jnp.stack([bbr, bbi]).reshape(2, nj, gl, p, S5_GROUP)
    bblk = jnp.einsum('rjgpi,gq->jrgiqp', bb, eye).reshape(nj, 2, LANES, gl * p)
    cc = jnp.stack([cr, ci]).reshape(2, nj, gl, S5_GROUP, p)
    cblk = jnp.einsum('rjghp,gq->jrgpqh', cc, eye).reshape(nj, 2, gl * p, LANES)
    pw = jnp.stack([pr, pi]).reshape(2, t + 1, nj, gl * p).transpose(2, 0, 1, 3)
    return bd, bblk, cblk, pw, pw.transpose(0, 1, 3, 2)


def _s5_scan(u, weights, d_skip):
    b, l, d = u.shape
    nj = weights[0].shape[0]
    n_chunks = l // S5_CHUNK
    tl = S5_CHUNK * LANES
    ns = 2 * weights[3].shape[3]
    col = pl.BlockSpec((None, l, LANES), lambda j, bi: (bi, 0, j))
    per_j = lambda w: pl.BlockSpec((None,) + w.shape[1:], lambda j, bi: (j,) + (0,) * (w.ndim - 1))
    return pl.pallas_call(
        functools.partial(_s5_scan_body, n_chunks=n_chunks),
        out_shape=jax.ShapeDtypeStruct((b, l, d), BF16),
        grid=(nj, b),
        in_specs=[col] + [per_j(w) for w in weights] + [pl.BlockSpec((1, LANES), lambda j, bi: (0, j))],
        out_specs=col,
        scratch_shapes=[pltpu.VMEM((tl, tl), BF16), pltpu.VMEM((tl, ns), BF16), pltpu.VMEM((ns, tl), BF16),
                        pltpu.VMEM((n_chunks, tl), BF16), pltpu.VMEM((n_chunks, ns), F32),
                        pltpu.VMEM((n_chunks, ns), F32), pltpu.VMEM((l, LANES), F32)],
        compiler_params=_params("arbitrary", "arbitrary"),
        name="s5_scan",
    )(u, *weights, d_skip.astype(F32)[None, :])


def _s5_out_body(g_ref, wg_ref, bg_ref, wo_ref, x_ref, o_ref):
    g = g_ref[...]
    gate = jax.nn.sigmoid(_dot(g, wg_ref[...]) + bg_ref[...])
    o_ref[...] = x_ref[...] + _dot((g.astype(F32) * gate).astype(BF16), wo_ref[...])


def _s5_out(g, w_glu, b_glu, w_out, x, tm=512):
    n, d = x.shape
    tm = min(tm, n)
    row = pl.BlockSpec((tm, d), lambda i: (i, 0))
    return pl.pallas_call(
        _s5_out_body,
        out_shape=jax.ShapeDtypeStruct((n, d), F32),
        grid=(n // tm,),
        in_specs=[row, _resident(w_glu.shape), _resident((1, d)), _resident(w_out.shape), row],
        out_specs=row,
        compiler_params=_params("parallel"),
        name="s5_glu_out",
    )(g, w_glu, b_glu.astype(F32)[None, :], w_out, x)


def _gmlp_body(x_ref, g_ref, win_ref, vg_ref, wsp_ref, bs_ref, wout_ref, o_ref, v_ref,
               *, half, heads):
    x = x_ref[...]
    tm = x.shape[0]
    hd = half // heads
    h = _rms(x, g_ref[...]).astype(BF16)
    per = 2
    wide = per * hd
    ss = jnp.zeros((tm, 1), F32)
    for c in range(heads // per):
        zv = _gelu(_dot(h, win_ref[:, half + c * wide:half + (c + 1) * wide]))
        ss = ss + jnp.sum(zv * zv, axis=-1, keepdims=True)
        v_ref[:, c * wide:(c + 1) * wide] = zv
    rinv = lax.rsqrt(ss * (1.0 / half) + EPS)
    row = lax.broadcasted_iota(jnp.int32, (GM_CHUNK, GM_CHUNK), 0)
    col = lax.broadcasted_iota(jnp.int32, (GM_CHUNK, GM_CHUNK), 1)

    def head_gate(head):
        sl = slice(head * hd, (head + 1) * hd)
        vh = (v_ref[:, sl] * rinv * vg_ref[:, sl]).astype(BF16)
        wm = jnp.where(col <= row, wsp_ref[head], 0.0).astype(BF16)
        bias = bs_ref[:, head:head + 1]
        return jnp.concatenate(
            [_dot(wm, vh[r * GM_CHUNK:(r + 1) * GM_CHUNK, :]) + bias for r in range(tm // GM_CHUNK)],
            axis=0)

    acc = x
    for c in range(heads // per):
        sl = slice(c * wide, (c + 1) * wide)
        zu = _gelu(_dot(h, win_ref[:, sl]))
        gate = jnp.concatenate([head_gate(c * per + r) for r in range(per)], axis=1)
        acc = acc + _dot((zu * gate).astype(BF16), wout_ref[sl, :])
    o_ref[...] = acc


def _gmlp(x, gain, w_in, v_gain, w_s, b_s, w_out, tm=256):
    n, d = x.shape
    tm = min(tm, n)
    half = w_out.shape[0]
    heads = w_s.shape[0]
    row = pl.BlockSpec((tm, d), lambda i: (i, 0))
    return pl.pallas_call(
        functools.partial(_gmlp_body, half=half, heads=heads),
        out_shape=jax.ShapeDtypeStruct((n, d), F32),
        grid=(n // tm,),
        in_specs=[row, _resident((1, d)), _resident(w_in.shape), _resident((1, half)),
                  _resident(w_s.shape), _resident((GM_CHUNK, heads)), _resident(w_out.shape)],
        out_specs=row,
        scratch_shapes=[pltpu.VMEM((tm, half), F32)],
        compiler_params=_params("parallel"),
        name="gmlp",
    )(x, gain[None, :], w_in, v_gain.astype(F32)[None, :], w_s.astype(F32), b_s.astype(F32).T, w_out)


def kernel(x, positions, norm_mix, norm_mlp, mlp_w1, mlp_w2, attn_w_qkv, attn_q_norm, attn_k_norm, attn_lambda, attn_sub_norm, attn_w_o, ssm_w_in, ssm_a_re, ssm_a_im, ssm_b_re, ssm_b_im, ssm_c_re, ssm_c_im, ssm_d, ssm_log_dt, ssm_w_glu, ssm_b_glu, ssm_w_out, gm_w_in, gm_v_norm, gm_w_s, gm_b_s, gm_w_out):
    b, l, d = x.shape
    n = b * l
    depth = norm_mix.shape[0]
    bf = lambda w: w.astype(BF16)
    xf = x.reshape(n, d)
    cos, sin = _rope_tables(positions)
    for i in range(depth):
        kind = i % N_MIXERS
        j = i // N_MIXERS
        if kind == 0:
            lambda_init = 0.8 - 0.6 * math.exp(-0.3 * i)
            q, k, v = _qkv_proj(xf, norm_mix[i], bf(attn_w_qkv[j]), attn_q_norm[j], attn_k_norm[j],
                                cos, sin)
            shape = (b, l, d)
            score_bound = (DA_HEAD_DIM ** 0.5 * jnp.max(jnp.abs(attn_q_norm[j]))
                           * jnp.max(jnp.abs(attn_k_norm[j])))
            o = _flash_attention(q.reshape(shape), k.reshape(shape), v.reshape(shape),
                                 attn_lambda[j], attn_sub_norm[j], score_bound, lambda_init)
            xf = _proj_residual(o.reshape(n, d), bf(attn_w_o[j]), xf)
        elif kind == 1:
            u = _norm_proj(xf, norm_mix[i], bf(ssm_w_in[j]))
            weights = _s5_weights(ssm_a_re[j], ssm_a_im[j], ssm_b_re[j], ssm_b_im[j],
                                  ssm_c_re[j], ssm_c_im[j], ssm_log_dt[j])
            g = _s5_scan(u.reshape(b, l, d), weights, ssm_d[j])
            xf = _s5_out(g.reshape(n, d), bf(ssm_w_glu[j]), ssm_b_glu[j], bf(ssm_w_out[j]), xf)
        else:
            xf = _gmlp(xf, norm_mix[i], bf(gm_w_in[j]), gm_v_norm[j], gm_w_s[j], gm_b_s[j],
                       bf(gm_w_out[j]))
        xf = _mlp(xf, norm_mlp[i], bf(mlp_w1[i]), bf(mlp_w2[i]))
    return xf.reshape(b, l, d)
```

```python
import functools
import math

import jax
import jax.numpy as jnp
from jax import lax
from jax.experimental import pallas as pl
from jax.experimental.pallas import tpu as pltpu

F32 = jnp.float32
BF16 = jnp.bfloat16

EPS = 1e-6
ROPE_THETA = 10000.0
N_MIXERS = 3

DA_HEADS = 8
DA_HEAD_DIM = 64
S5_GROUP = 16
S5_STATE = 64
S5_CHUNK = 16
GM_HEADS = 8
GM_CHUNK = 128

LANES = 128
MXU_DIM = 256
VMEM_LIMIT = 56 * 1024 * 1024
NEG_BIG = -1e30
FAST_SCORE_BOUND = 30.0
LOG2E = math.log2(math.e)


def _params(*sem):
    return pltpu.CompilerParams(dimension_semantics=sem, vmem_limit_bytes=VMEM_LIMIT)


def _resident(shape):
    zeros = (0,) * len(shape)
    return pl.BlockSpec(shape, lambda *_: zeros, pipeline_mode=pl.Buffered(1))


def _rms(x, gain):
    return x * lax.rsqrt(jnp.mean(x * x, axis=-1, keepdims=True) + EPS) * gain


def _gelu(x):
    c = math.sqrt(2.0 / math.pi)
    return 0.5 * x * (1.0 + jnp.tanh(c * (x + 0.044715 * (x * x * x))))


def _dot(a, b):
    return jnp.dot(a, b, preferred_element_type=F32)


def _rope_body(pos_ref, freq_ref, cos_ref, sin_ref):
    ang = pos_ref[...].astype(F32) * freq_ref[...]
    lane = lax.broadcasted_iota(jnp.int32, ang.shape, 1)
    s = jnp.sin(ang)
    cos_ref[...] = jnp.cos(ang)
    sin_ref[...] = jnp.where(lane % DA_HEAD_DIM < DA_HEAD_DIM // 2, -s, s)


def _rope_tables(positions):
    n = positions.size
    tm = min(n, 2048)
    inv_freq = ROPE_THETA ** (-jnp.arange(0, DA_HEAD_DIM, 2, dtype=F32) / DA_HEAD_DIM)
    freq = jnp.tile(inv_freq, LANES // (DA_HEAD_DIM // 2))[None, :]
    return pl.pallas_call(
        _rope_body,
        out_shape=(jax.ShapeDtypeStruct((n, LANES), F32),) * 2,
        grid=(n // tm,),
        in_specs=[pl.BlockSpec((tm, 1), lambda i: (i, 0)), _resident((1, LANES))],
        out_specs=(pl.BlockSpec((tm, LANES), lambda i: (i, 0)),) * 2,
        compiler_params=_params("parallel"),
        name="rope_tables",
    )(positions.reshape(n, 1), freq)


def _qkv_body(x_ref, g_ref, w_ref, qg_ref, kg_ref, cos_ref, sin_ref, bd_ref,
              q_ref, k_ref, v_ref, *, d):
    h = _rms(x_ref[...], g_ref[...]).astype(BF16)
    cos = cos_ref[...]
    sin = sin_ref[...]
    lane = lax.broadcasted_iota(jnp.int32, cos.shape, 1)
    first_half = lane % DA_HEAD_DIM < DA_HEAD_DIM // 2
    bd = bd_ref[...]
    seg = bd.shape[0]

    def norm_rope(col0, gain_ref, out_ref, scale):
        acc = _dot(h, w_ref[:, col0:col0 + d])
        sq = (acc * acc).astype(BF16)
        gain = gain_ref[...]
        for c in range(d // seg):
            ss = _dot(sq[:, c * seg:(c + 1) * seg], bd)
            r = lax.rsqrt(ss * (1.0 / DA_HEAD_DIM) + EPS)
            for hh in range(seg // LANES):
                lo = c * seg + hh * LANES
                qn = acc[:, lo:lo + LANES] * r[:, hh * LANES:(hh + 1) * LANES] * gain
                partner = jnp.where(first_half,
                                    pltpu.roll(qn, LANES - DA_HEAD_DIM // 2, 1),
                                    pltpu.roll(qn, DA_HEAD_DIM // 2, 1))
                out = qn * cos + partner * sin
                out_ref[:, lo:lo + LANES] = (out * scale).astype(BF16)

    norm_rope(0, qg_ref, q_ref, DA_HEAD_DIM ** -0.5 * LOG2E)
    norm_rope(d, kg_ref, k_ref, 1.0)
    v_ref[...] = _dot(h, w_ref[:, 2 * d:3 * d]).astype(BF16)


def _qkv_proj(x, gain, w_qkv, q_gain, k_gain, cos, sin, tm=512):
    n, d = x.shape
    tm = min(tm, n)
    seg = MXU_DIM
    r = jnp.arange(seg) // DA_HEAD_DIM
    bd = (r[:, None] == r[None, :]).astype(BF16)
    tile = lambda g: jnp.tile(g.astype(F32), LANES // DA_HEAD_DIM)[None, :]
    row = pl.BlockSpec((tm, d), lambda i: (i, 0))
    tab = pl.BlockSpec((tm, LANES), lambda i: (i, 0))
    return pl.pallas_call(
        functools.partial(_qkv_body, d=d),
        out_shape=(jax.ShapeDtypeStruct((n, d), BF16),) * 3,
        grid=(n // tm,),
        in_specs=[row, _resident((1, d)), _resident((d, 3 * d)), _resident((1, LANES)),
                  _resident((1, LANES)), tab, tab, _resident((seg, seg))],
        out_specs=(row,) * 3,
        compiler_params=_params("parallel"),
        name="qkv_proj",
    )(x, gain[None, :], w_qkv, tile(q_gain), tile(k_gain), cos, sin, bd)


def _split_components(q):
    lane = lax.broadcasted_iota(jnp.int32, q.shape, 1)
    zero = jnp.zeros_like(q)
    return jnp.where(lane < DA_HEAD_DIM, q, zero), jnp.where(lane >= DA_HEAD_DIM, q, zero)


def _scores(qc, kb):
    return lax.dot_general(qc, kb, (((1,), (1,)), ((), ())), preferred_element_type=F32)


def _causal(shape):
    row = lax.broadcasted_iota(jnp.int32, shape, 0)
    col = lax.broadcasted_iota(jnp.int32, shape, 1)
    return col <= row


def _diff_combine(o1, o2, lam_ref, sg_ref, lambda_init):
    lam = lam_ref[...]
    dot_sum = lambda a, b: jnp.sum(lam[a:a + 1] * lam[b:b + 1], axis=-1, keepdims=True)
    lam_full = jnp.exp(dot_sum(0, 1)) - jnp.exp(dot_sum(2, 3)) + lambda_init
    o = o1 - lam_full * o2
    return (_rms(o, sg_ref[...]) * (1.0 - lambda_init)).astype(BF16)


def _flash_safe_body(lam_ref, sg_ref, q_ref, k_ref, v_ref, o_ref, acc1_ref, acc2_ref,
                     *, tq, lambda_init):
    i = pl.program_id(2)
    qs = _split_components(q_ref[...])
    accs = (acc1_ref, acc2_ref)
    acc1_ref[...] = jnp.zeros_like(acc1_ref)
    acc2_ref[...] = jnp.zeros_like(acc2_ref)

    def step(j, carry, masked):
        kb = k_ref[pl.ds(j * tq, tq), :]
        vb = v_ref[pl.ds(j * tq, tq), :]
        out = []
        for c in range(2):
            m, l = carry[2 * c], carry[2 * c + 1]
            s = _scores(qs[c], kb)
            if masked:
                s = jnp.where(_causal(s.shape), s, NEG_BIG)
            m_new = jnp.maximum(m, jnp.max(s, axis=-1, keepdims=True))
            alpha = jnp.exp2(m - m_new)
            p = jnp.exp2(s - m_new)
            l = alpha * l + jnp.sum(p, axis=-1, keepdims=True)
            accs[c][...] = alpha * accs[c][...] + _dot(p.astype(BF16), vb)
            out += [m_new, l]
        return tuple(out)

    init = (jnp.full((tq, 1), NEG_BIG, F32), jnp.zeros((tq, 1), F32)) * 2
    carry = lax.fori_loop(0, i, lambda j, c: step(j, c, False), init)
    _, l1, _, l2 = step(i, carry, True)
    o_ref[...] = _diff_combine(acc1_ref[...] / l1, acc2_ref[...] / l2, lam_ref, sg_ref, lambda_init)


def _flash_fast_body(lam_ref, sg_ref, q_ref, k_ref, v_ref, o_ref, acc_ref, p_ref,
                     *, tq, heads, lambda_init):
    i = pl.program_id(2)
    lanes = lambda h: slice(h * LANES, (h + 1) * LANES)
    q2 = [jnp.concatenate(_split_components(q_ref[:, lanes(h)]), axis=0) for h in range(heads)]
    lane = lax.broadcasted_iota(jnp.int32, (tq, LANES), 1)
    ones_col = jnp.where(lane == 0, 1.0, 0.0).astype(BF16)
    row = lax.broadcasted_iota(jnp.int32, (2 * tq, tq), 0)
    col = lax.broadcasted_iota(jnp.int32, (2 * tq, tq), 1)
    delta = col - jnp.where(row >= tq, row - tq, row)

    def probs(h, j):
        p = jnp.exp2(_scores(q2[h], k_ref[pl.ds(j * tq, tq), lanes(h)]))
        return jnp.where(delta <= (i - j) * tq, p, 0.0).astype(BF16)

    def weighted_values(h, j):
        vb = jnp.concatenate([v_ref[pl.ds(j * tq, tq), lanes(h)], ones_col], axis=1)
        return _dot(p_ref[h], vb)

    for h in range(heads):
        p_ref[h] = probs(h, 0)
    acc_ref[...] = jnp.zeros_like(acc_ref)

    def body(j, carry):
        for h in range(heads):
            acc_ref[h] += weighted_values(h, j - 1)
            p_ref[h] = probs(h, j)
        return carry

    lax.fori_loop(1, i + 1, body, 0)
    for h in range(heads):
        acc = acc_ref[h] + weighted_values(h, i)
        o = acc[:, :LANES] / acc[:, LANES:LANES + 1]
        o_ref[:, lanes(h)] = _diff_combine(o[:tq], o[tq:], lam_ref, sg_ref, lambda_init)


def _flash_call(body, name, scratch, heads_per_step, q, k, v, lam, sub_gain, lambda_init, tq):
    b, l, d = q.shape
    width = heads_per_step * LANES
    qspec = pl.BlockSpec((None, tq, width), lambda bi, hi, i: (bi, i, hi))
    kvspec = pl.BlockSpec((None, l, width), lambda bi, hi, i: (bi, 0, hi))
    return pl.pallas_call(
        functools.partial(body, tq=tq, lambda_init=lambda_init),
        out_shape=jax.ShapeDtypeStruct((b, l, d), BF16),
        grid=(b, d // width, l // tq),
        in_specs=[_resident(lam.shape), _resident((1, LANES)), qspec, kvspec, kvspec],
        out_specs=qspec,
        scratch_shapes=scratch,
        compiler_params=_params("parallel", "parallel", "arbitrary"),
        name=name,
    )(lam, sub_gain, q, k, v)


def _flash_attention(q, k, v, lam, sub_gain, score_bound, lambda_init, tq_fast=512, tq_safe=256, hps=4):
    tqf = min(tq_fast, q.shape[1])
    tqs = min(tq_safe, q.shape[1])
    args = (q, k, v, lam.astype(F32), sub_gain.astype(F32)[None, :])
    fast_scratch = [pltpu.VMEM((hps, 2 * tqf, 2 * LANES), F32), pltpu.VMEM((hps, 2 * tqf, tqf), BF16)]
    safe_scratch = [pltpu.VMEM((tqs, LANES), F32)] * 2
    fast = lambda *a: _flash_call(functools.partial(_flash_fast_body, heads=hps), "flash_fast", fast_scratch,
                                  hps, *a, lambda_init, tqf)
    safe = lambda *a: _flash_call(_flash_safe_body, "flash_safe", safe_scratch, 1, *a, lambda_init, tqs)
    return lax.cond(score_bound <= FAST_SCORE_BOUND, fast, safe, *args)


def _norm_proj_body(x_ref, g_ref, w_ref, o_ref):
    o_ref[...] = _dot(_rms(x_ref[...], g_ref[...]).astype(BF16), w_ref[...])


def _norm_proj(x, gain, w, tm=512):
    n, d = x.shape
    tm = min(tm, n)
    return pl.pallas_call(
        _norm_proj_body,
        out_shape=jax.ShapeDtypeStruct((n, w.shape[1]), F32),
        grid=(n // tm,),
        in_specs=[pl.BlockSpec((tm, d), lambda i: (i, 0)), _resident((1, d)), _resident(w.shape)],
        out_specs=pl.BlockSpec((tm, w.shape[1]), lambda i: (i, 0)),
        compiler_params=_params("parallel"),
        name="norm_proj",
    )(x, gain[None, :], w)


def _relu2_mlp(x, g_ref, w1_ref, w2_ref, th):
    h = _rms(x, g_ref[...]).astype(BF16)
    acc = x
    for j in range(w1_ref.shape[1] // th):
        a = _dot(h, w1_ref[:, j * th:(j + 1) * th])
        a = jnp.square(jnp.maximum(a, 0.0)).astype(BF16)
        acc = acc + _dot(a, w2_ref[j * th:(j + 1) * th, :])
    return acc


def _mlp_body(x_ref, g_ref, w1_ref, w2_ref, o_ref, *, th):
    o_ref[...] = _relu2_mlp(x_ref[...], g_ref, w1_ref, w2_ref, th)


def _proj_mlp_body(a_ref, wp_ref, x_ref, g_ref, w1_ref, w2_ref, o_ref, *, th):
    x = x_ref[...] + _dot(a_ref[...], wp_ref[...])
    o_ref[...] = _relu2_mlp(x, g_ref, w1_ref, w2_ref, th)


def _glu_mlp_body(a_ref, wg_ref, bg_ref, wp_ref, x_ref, g_ref, w1_ref, w2_ref, o_ref, *, th):
    a = a_ref[...]
    gate = jax.nn.sigmoid(_dot(a, wg_ref[...]) + bg_ref[...])
    x = x_ref[...] + _dot((a.astype(F32) * gate).astype(BF16), wp_ref[...])
    o_ref[...] = _relu2_mlp(x, g_ref, w1_ref, w2_ref, th)


def _mlp(x, gain, w1, w2, proj=None, glu=None, tm=512, th=1024):
    n, d = x.shape
    tm = min(tm, n)
    row = pl.BlockSpec((tm, d), lambda i: (i, 0))
    pre, pre_specs, body, name = (), [], _mlp_body, "relu2_mlp"
    if proj is not None:
        a, wp = proj
        pre, body, name = (a, wp), _proj_mlp_body, "proj_relu2_mlp"
        pre_specs = [pl.BlockSpec((tm, a.shape[1]), lambda i: (i, 0)), _resident(wp.shape)]
    elif glu is not None:
        a, wg, bg, wp = glu
        pre, body, name = (a, wg, bg.astype(F32)[None, :], wp), _glu_mlp_body, "glu_relu2_mlp"
        pre_specs = [row, _resident(wg.shape), _resident((1, d)), _resident(wp.shape)]
    return pl.pallas_call(
        functools.partial(body, th=th),
        out_shape=jax.ShapeDtypeStruct((n, d), F32),
        grid=(n // tm,),
        in_specs=pre_specs + [row, _resident((1, d)), _resident(w1.shape), _resident(w2.shape)],
        out_specs=row,
        compiler_params=_params("parallel"),
        name=name,
    )(*pre, x, gain[None, :], w1, w2)


def _cast_body(w_ref, o_ref):
    o_ref[...] = w_ref[...].astype(BF16)


def _to_bf16(w, tr=256):
    cols = w.shape[-1]
    w2 = w.reshape(-1, cols)
    blk = pl.BlockSpec((tr, cols), lambda i: (i, 0))
    out = pl.pallas_call(
        _cast_body,
        out_shape=jax.ShapeDtypeStruct(w2.shape, BF16),
        grid=(w2.shape[0] // tr,),
        in_specs=[blk],
        out_specs=blk,
        compiler_params=_params("parallel"),
        name="to_bf16",
    )(w2)
    return out.reshape(w.shape)


def _s5_build_operators(bd_ref, bb_ref, cb_ref, pw_ref, pwc_ref, m_ref, ws_ref, wo_ref):
    t = S5_CHUNK
    half = pw_ref.shape[2]
    m_ref[...] = jnp.zeros_like(m_ref)
    for s in range(t):
        for tt in range(s, t):
            m_ref[s * LANES:(s + 1) * LANES, tt * LANES:(tt + 1) * LANES] = bd_ref[tt - s]
    bbr, bbi = bb_ref[0], bb_ref[1]
    for s in range(t):
        pr = pw_ref[0, t - 1 - s:t - s, :]
        pi = pw_ref[1, t - 1 - s:t - s, :]
        ws_ref[s * LANES:(s + 1) * LANES, 0:half] = (bbr * pr - bbi * pi).astype(BF16)
        ws_ref[s * LANES:(s + 1) * LANES, half:2 * half] = (bbr * pi + bbi * pr).astype(BF16)
    cbr, cbi = cb_ref[0], cb_ref[1]
    for tt in range(t):
        pr = pwc_ref[0, :, tt + 1:tt + 2]
        pi = pwc_ref[1, :, tt + 1:tt + 2]
        wo_ref[0:half, tt * LANES:(tt + 1) * LANES] = (cbr * pr - cbi * pi).astype(BF16)
        wo_ref[half:2 * half, tt * LANES:(tt + 1) * LANES] = (-(cbr * pi + cbi * pr)).astype(BF16)


def _s5_scan_body(u_ref, bd_ref, bb_ref, cb_ref, pw_ref, pwc_ref, d_ref, o_ref,
                  m_ref, ws_ref, wo_ref, u2_ref, s_ref, xp_ref, y_ref, *, n_chunks):
    t = S5_CHUNK
    half = pw_ref.shape[2]

    @pl.when(pl.program_id(1) == 0)
    def _():
        _s5_build_operators(bd_ref, bb_ref, cb_ref, pw_ref, pwc_ref, m_ref, ws_ref, wo_ref)

    for s in range(t):
        u2_ref[:, s * LANES:(s + 1) * LANES] = u_ref[pl.ds(s, n_chunks, stride=t), :].astype(BF16)
    s_ref[...] = _dot(u2_ref[...], ws_ref[...])
    ar = pw_ref[0, t:t + 1, :]
    ai = pw_ref[1, t:t + 1, :]

    def carry_step(c, carry):
        xr, xi = carry
        xp_ref[pl.ds(c, 1), 0:half] = xr
        xp_ref[pl.ds(c, 1), half:2 * half] = xi
        sr = s_ref[pl.ds(c, 1), 0:half]
        si = s_ref[pl.ds(c, 1), half:2 * half]
        return ar * xr - ai * xi + sr, ar * xi + ai * xr + si

    zero = jnp.zeros((1, half), F32)
    lax.fori_loop(0, n_chunks, carry_step, (zero, zero))
    xp = xp_ref[...].astype(BF16)
    per = MXU_DIM // LANES
    for cb in range(t // per):
        cols = slice(cb * MXU_DIM, (cb + 1) * MXU_DIM)
        kdim = (cb + 1) * MXU_DIM
        y2 = _dot(u2_ref[:, 0:kdim], m_ref[0:kdim, cols]) + _dot(xp, wo_ref[:, cols])
        for r in range(per):
            y_ref[pl.ds(cb * per + r, n_chunks, stride=t), :] = y2[:, r * LANES:(r + 1) * LANES]
    y = y_ref[...] + d_ref[...] * u_ref[...]
    o_ref[...] = _gelu(y).astype(BF16)


def _s5_weights(a_re, a_im, b_re, b_im, c_re, c_im, log_dt):
    t = S5_CHUNK
    g, p = a_re.shape
    gl = LANES // S5_GROUP
    nj = g // gl
    hp = lax.Precision.HIGHEST
    lr = jnp.minimum(a_re.astype(F32), -1e-4)
    li = a_im.astype(F32)
    dt = jnp.exp(log_dt.astype(F32))[:, None]
    steps = jnp.arange(t + 1, dtype=F32)[:, None, None]
    mag = jnp.exp(steps * (lr * dt))
    pr = mag * jnp.cos(steps * (li * dt))
    pi = mag * jnp.sin(steps * (li * dt))
    nr, ni = pr[1] - 1.0, pi[1]
    den = lr * lr + li * li
    zr = ((nr * lr + ni * li) / den)[..., None]
    zi = ((ni * lr - nr * li) / den)[..., None]
    br, bi = b_re.astype(F32), b_im.astype(F32)
    bbr = zr * br - zi * bi
    bbi = zr * bi + zi * br
    cr, ci = c_re.astype(F32), c_im.astype(F32)
    wr = pr[:t, :, :, None] * bbr[None] - pi[:t, :, :, None] * bbi[None]
    wi = pr[:t, :, :, None] * bbi[None] + pi[:t, :, :, None] * bbr[None]
    kern = (jnp.einsum('ghp,kgpi->kgih', cr, wr, precision=hp)
            - jnp.einsum('ghp,kgpi->kgih', ci, wi, precision=hp))
    eye = jnp.eye(gl, dtype=F32)
    kern = kern.reshape(t, nj, gl, S5_GROUP, S5_GROUP)
    bd = jnp.einsum('kjgih,gq->jkgiqh', kern, eye).reshape(nj, t, LANES, LANES).astype(BF16)
    bb = jnp.stack([bbr, bbi]).reshape(2, nj, gl, p, S5_GROUP)
    bblk = jnp.einsum('rjgpi,gq->jrgiqp', bb, eye).reshape(nj, 2, LANES, gl * p)
    cc = jnp.stack([cr, ci]).reshape(2, nj, gl, S5_GROUP, p)
    cblk = jnp.einsum('rjghp,gq->jrgpqh', cc, eye).reshape(nj, 2, gl * p, LANES)
    pw = jnp.stack([pr, pi]).reshape(2, t + 1, nj, gl * p).transpose(2, 0, 1, 3)
    return bd, bblk, cblk, pw, pw.transpose(0, 1, 3, 2)


def _s5_scan(u, weights, d_skip):
    b, l, d = u.shape
    nj = weights[0].shape[0]
    n_chunks = l // S5_CHUNK
    tl = S5_CHUNK * LANES
    ns = 2 * weights[3].shape[3]
    col = pl.BlockSpec((None, l, LANES), lambda j, bi: (bi, 0, j))
    per_j = lambda w: pl.BlockSpec((None,) + w.shape[1:], lambda j, bi: (j,) + (0,) * (w.ndim - 1))
    return pl.pallas_call(
        functools.partial(_s5_scan_body, n_chunks=n_chunks),
        out_shape=jax.ShapeDtypeStruct((b, l, d), BF16),
        grid=(nj, b),
        in_specs=[col] + [per_j(w) for w in weights] + [pl.BlockSpec((1, LANES), lambda j, bi: (0, j))],
        out_specs=col,
        scratch_shapes=[pltpu.VMEM((tl, tl), BF16), pltpu.VMEM((tl, ns), BF16), pltpu.VMEM((ns, tl), BF16),
                        pltpu.VMEM((n_chunks, tl), BF16), pltpu.VMEM((n_chunks, ns), F32),
                        pltpu.VMEM((n_chunks, ns), F32), pltpu.VMEM((l, LANES), F32)],
        compiler_params=_params("arbitrary", "arbitrary"),
        name="s5_scan",
    )(u, *weights, d_skip.astype(F32)[None, :])


def _gmlp_body(x_ref, g_ref, win_ref, vg_ref, wsp_ref, bs_ref, wout_ref, o_ref, v_ref,
               *, half, heads):
    x = x_ref[...]
    tm = x.shape[0]
    hd = half // heads
    h = _rms(x, g_ref[...]).astype(BF16)
    per = 2
    wide = per * hd
    ss = jnp.zeros((tm, 1), F32)
    for c in range(heads // per):
        zv = _gelu(_dot(h, win_ref[:, half + c * wide:half + (c + 1) * wide]))
        ss = ss + jnp.sum(zv * zv, axis=-1, keepdims=True)
        v_ref[:, c * wide:(c + 1) * wide] = zv
    rinv = lax.rsqrt(ss * (1.0 / half) + EPS)
    row = lax.broadcasted_iota(jnp.int32, (GM_CHUNK, GM_CHUNK), 0)
    col = lax.broadcasted_iota(jnp.int32, (GM_CHUNK, GM_CHUNK), 1)

    def head_gate(head):
        sl = slice(head * hd, (head + 1) * hd)
        vh = (v_ref[:, sl] * rinv * vg_ref[:, sl]).astype(BF16)
        wm = jnp.where(col <= row, wsp_ref[head], 0.0).astype(BF16)
        bias = bs_ref[:, head:head + 1]
        return jnp.concatenate(
            [_dot(wm, vh[r * GM_CHUNK:(r + 1) * GM_CHUNK, :]) + bias for r in range(tm // GM_CHUNK)],
            axis=0)

    acc = x
    for c in range(heads // per):
        sl = slice(c * wide, (c + 1) * wide)
        zu = _gelu(_dot(h, win_ref[:, sl]))
        gate = jnp.concatenate([head_gate(c * per + r) for r in range(per)], axis=1)
        acc = acc + _dot((zu * gate).astype(BF16), wout_ref[sl, :])
    o_ref[...] = acc


def _gmlp(x, gain, w_in, v_gain, w_s, b_s, w_out, tm=256):
    n, d = x.shape
    tm = min(tm, n)
    half = w_out.shape[0]
    heads = w_s.shape[0]
    row = pl.BlockSpec((tm, d), lambda i: (i, 0))
    return pl.pallas_call(
        functools.partial(_gmlp_body, half=half, heads=heads),
        out_shape=jax.ShapeDtypeStruct((n, d), F32),
        grid=(n // tm,),
        in_specs=[row, _resident((1, d)), _resident(w_in.shape), _resident((1, half)),
                  _resident(w_s.shape), _resident((GM_CHUNK, heads)), _resident(w_out.shape)],
        out_specs=row,
        scratch_shapes=[pltpu.VMEM((tm, half), F32)],
        compiler_params=_params("parallel"),
        name="gmlp",
    )(x, gain[None, :], w_in, v_gain.astype(F32)[None, :], w_s.astype(F32), b_s.astype(F32).T, w_out)


def kernel(x, positions, norm_mix, norm_mlp, mlp_w1, mlp_w2, attn_w_qkv, attn_q_norm, attn_k_norm, attn_lambda, attn_sub_norm, attn_w_o, ssm_w_in, ssm_a_re, ssm_a_im, ssm_b_re, ssm_b_im, ssm_c_re, ssm_c_im, ssm_d, ssm_log_dt, ssm_w_glu, ssm_b_glu, ssm_w_out, gm_w_in, gm_v_norm, gm_w_s, gm_b_s, gm_w_out):
    b, l, d = x.shape
    n = b * l
    depth = norm_mix.shape[0]
    xf = x.reshape(n, d)
    cos, sin = _rope_tables(positions)
    mlp_w1, mlp_w2, attn_w_qkv, attn_w_o = map(_to_bf16, (mlp_w1, mlp_w2, attn_w_qkv, attn_w_o))
    ssm_w_in, ssm_w_glu, ssm_w_out, gm_w_in, gm_w_out = map(
        _to_bf16, (ssm_w_in, ssm_w_glu, ssm_w_out, gm_w_in, gm_w_out))
    for i in range(depth):
        kind = i % N_MIXERS
        j = i // N_MIXERS
        mlp = functools.partial(_mlp, xf, norm_mlp[i], mlp_w1[i], mlp_w2[i])
        if kind == 0:
            lambda_init = 0.8 - 0.6 * math.exp(-0.3 * i)
            q, k, v = _qkv_proj(xf, norm_mix[i], attn_w_qkv[j], attn_q_norm[j], attn_k_norm[j], cos, sin)
            shape = (b, l, d)
            score_bound = (DA_HEAD_DIM ** 0.5 * jnp.max(jnp.abs(attn_q_norm[j]))
                           * jnp.max(jnp.abs(attn_k_norm[j])))
            o = _flash_attention(q.reshape(shape), k.reshape(shape), v.reshape(shape),
                                 attn_lambda[j], attn_sub_norm[j], score_bound, lambda_init)
            xf = mlp(proj=(o.reshape(n, d), attn_w_o[j]))
        elif kind == 1:
            u = _norm_proj(xf, norm_mix[i], ssm_w_in[j])
            weights = _s5_weights(ssm_a_re[j], ssm_a_im[j], ssm_b_re[j], ssm_b_im[j],
                                  ssm_c_re[j], ssm_c_im[j], ssm_log_dt[j])
            g = _s5_scan(u.reshape(b, l, d), weights, ssm_d[j])
            xf = mlp(glu=(g.reshape(n, d), ssm_w_glu[j], ssm_b_glu[j], ssm_w_out[j]))
        else:
            xf = _gmlp(xf, norm_mix[i], gm_w_in[j], gm_v_norm[j], gm_w_s[j], gm_b_s[j], gm_w_out[j])
            xf = _mlp(xf, norm_mlp[i], mlp_w1[i], mlp_w2[i])
    return xf.reshape(b, l, d)
```

```python
import functools
import math
from typing import NamedTuple

import jax
import jax.numpy as jnp
from jax import lax
from jax.experimental import pallas as pl
from jax.experimental.pallas import tpu as pltpu

F32 = jnp.float32
BF16 = jnp.bfloat16

EPS = 1e-6
ROPE_THETA = 10000.0
N_MIXERS = 3

DA_HEADS = 8
DA_HEAD_DIM = 64
S5_GROUP = 16
S5_STATE = 64
S5_CHUNK = 16
GM_HEADS = 8
GM_CHUNK = 128

LANES = 128
MXU_DIM = 256
ONES_ROWS = 16
VMEM_LIMIT = 56 * 1024 * 1024
NEG_BIG = -1e30
FAST_SCORE_BOUND = 30.0
LOG2E = math.log2(math.e)


def _params(*sem):
    return pltpu.CompilerParams(dimension_semantics=sem, vmem_limit_bytes=VMEM_LIMIT)


def _resident(shape):
    zeros = (0,) * len(shape)
    return pl.BlockSpec(shape, lambda *_: zeros, pipeline_mode=pl.Buffered(1))


class _Layer(NamedTuple):
    stack: jax.Array
    index: int

    @property
    def shape(self):
        return self.stack.shape[1:]

    def spec(self):
        idx = (self.index,) + (0,) * len(self.shape)
        return pl.BlockSpec((None,) + self.shape, lambda *_: idx, pipeline_mode=pl.Buffered(1))


def _rms(x, gain):
    return x * lax.rsqrt(jnp.mean(x * x, axis=-1, keepdims=True) + EPS) * gain


def _gelu(x):
    c = math.sqrt(2.0 / math.pi)
    return 0.5 * x * (1.0 + jnp.tanh(c * (x + 0.044715 * (x * x * x))))


def _dot(a, b):
    return jnp.dot(a, b, preferred_element_type=F32)


def _rope_body(pos_ref, freq_ref, cos_ref, sin_ref):
    ang = pos_ref[...].astype(F32) * freq_ref[...]
    lane = lax.broadcasted_iota(jnp.int32, ang.shape, 1)
    s = jnp.sin(ang)
    cos_ref[...] = jnp.cos(ang)
    sin_ref[...] = jnp.where(lane % DA_HEAD_DIM < DA_HEAD_DIM // 2, -s, s)


def _rope_tables(positions):
    n = positions.size
    tm = min(n, 2048)
    inv_freq = ROPE_THETA ** (-jnp.arange(0, DA_HEAD_DIM, 2, dtype=F32) / DA_HEAD_DIM)
    freq = jnp.tile(inv_freq, LANES // (DA_HEAD_DIM // 2))[None, :]
    return pl.pallas_call(
        _rope_body,
        out_shape=(jax.ShapeDtypeStruct((n, LANES), F32),) * 2,
        grid=(n // tm,),
        in_specs=[pl.BlockSpec((tm, 1), lambda i: (i, 0)), _resident((1, LANES))],
        out_specs=(pl.BlockSpec((tm, LANES), lambda i: (i, 0)),) * 2,
        compiler_params=_params("parallel"),
        name="rope_tables",
    )(positions.reshape(n, 1), freq)


def _qkv_body(x_ref, g_ref, w_ref, qg_ref, kg_ref, cos_ref, sin_ref, bd_ref,
              q_ref, k_ref, v_ref, vt_ref, *, d):
    h = _rms(x_ref[...], g_ref[...]).astype(BF16)
    cos = cos_ref[...]
    sin = sin_ref[...]
    lane = lax.broadcasted_iota(jnp.int32, cos.shape, 1)
    first_half = lane % DA_HEAD_DIM < DA_HEAD_DIM // 2
    bd = bd_ref[...]
    seg = bd.shape[0]

    def norm_rope(col0, gain_ref, out_ref, scale):
        acc = _dot(h, w_ref[:, col0:col0 + d])
        sq = (acc * acc).astype(BF16)
        gain = gain_ref[...]
        for c in range(d // seg):
            ss = _dot(sq[:, c * seg:(c + 1) * seg], bd)
            r = lax.rsqrt(ss * (1.0 / DA_HEAD_DIM) + EPS)
            for hh in range(seg // LANES):
                lo = c * seg + hh * LANES
                qn = acc[:, lo:lo + LANES] * r[:, hh * LANES:(hh + 1) * LANES] * gain
                partner = jnp.where(first_half,
                                    pltpu.roll(qn, LANES - DA_HEAD_DIM // 2, 1),
                                    pltpu.roll(qn, DA_HEAD_DIM // 2, 1))
                out = qn * cos + partner * sin
                out_ref[:, lo:lo + LANES] = (out * scale).astype(BF16)

    norm_rope(0, qg_ref, q_ref, DA_HEAD_DIM ** -0.5 * LOG2E)
    norm_rope(d, kg_ref, k_ref, 1.0)
    v = _dot(h, w_ref[:, 2 * d:3 * d])
    v_ref[...] = v.astype(BF16)
    for hh in range(d // LANES):
        vt_ref[hh] = v[:, hh * LANES:(hh + 1) * LANES].T.astype(BF16)


def _qkv_proj(x, gain, w_qkv, q_gain, k_gain, cos, sin, batch, tm=512):
    n, d = x.shape
    tm = min(tm, n // batch)
    tiles = n // batch // tm
    seg = MXU_DIM
    r = jnp.arange(seg) // DA_HEAD_DIM
    bd = (r[:, None] == r[None, :]).astype(BF16)
    tile = lambda g: jnp.tile(g.astype(F32), LANES // DA_HEAD_DIM)[None, :]
    row = pl.BlockSpec((tm, d), lambda i: (i, 0))
    tab = pl.BlockSpec((tm, LANES), lambda i: (i, 0))
    return pl.pallas_call(
        functools.partial(_qkv_body, d=d),
        out_shape=(jax.ShapeDtypeStruct((n, d), BF16),) * 3
        + (jax.ShapeDtypeStruct((batch, d // LANES, LANES, n // batch), BF16),),
        grid=(n // tm,),
        in_specs=[row, _resident((1, d)), w_qkv.spec(), _resident((1, LANES)),
                  _resident((1, LANES)), tab, tab, _resident((seg, seg))],
        out_specs=(row,) * 3
        + (pl.BlockSpec((None, d // LANES, LANES, tm), lambda i: (i // tiles, 0, 0, i % tiles)),),
        compiler_params=_params("parallel"),
        name="qkv_proj",
    )(x, gain[None, :], w_qkv.stack, tile(q_gain), tile(k_gain), cos, sin, bd)


def _split_components(q):
    lane = lax.broadcasted_iota(jnp.int32, q.shape, 1)
    zero = jnp.zeros_like(q)
    return jnp.where(lane < DA_HEAD_DIM, q, zero), jnp.where(lane >= DA_HEAD_DIM, q, zero)


def _scores(qc, kb):
    return lax.dot_general(qc, kb, (((1,), (1,)), ((), ())), preferred_element_type=F32)


def _causal(shape):
    row = lax.broadcasted_iota(jnp.int32, shape, 0)
    col = lax.broadcasted_iota(jnp.int32, shape, 1)
    return col <= row


def _diff_combine(o1, o2, lam_ref, sg_ref, lambda_init):
    lam = lam_ref[...]
    dot_sum = lambda a, b: jnp.sum(lam[a:a + 1] * lam[b:b + 1], axis=-1, keepdims=True)
    lam_full = jnp.exp(dot_sum(0, 1)) - jnp.exp(dot_sum(2, 3)) + lambda_init
    o = o1 - lam_full * o2
    return (_rms(o, sg_ref[...]) * (1.0 - lambda_init)).astype(BF16)


def _flash_safe_body(lam_ref, sg_ref, q_ref, k_ref, v_ref, o_ref, acc1_ref, acc2_ref,
                     *, tq, lambda_init):
    i = pl.program_id(2)
    qs = _split_components(q_ref[...])
    accs = (acc1_ref, acc2_ref)
    acc1_ref[...] = jnp.zeros_like(acc1_ref)
    acc2_ref[...] = jnp.zeros_like(acc2_ref)

    def step(j, carry, masked):
        kb = k_ref[pl.ds(j * tq, tq), :]
        vb = v_ref[pl.ds(j * tq, tq), :]
        out = []
        for c in range(2):
            m, l = carry[2 * c], carry[2 * c + 1]
            s = _scores(qs[c], kb)
            if masked:
                s = jnp.where(_causal(s.shape), s, NEG_BIG)
            m_new = jnp.maximum(m, jnp.max(s, axis=-1, keepdims=True))
            alpha = jnp.exp2(m - m_new)
            p = jnp.exp2(s - m_new)
            l = alpha * l + jnp.sum(p, axis=-1, keepdims=True)
            accs[c][...] = alpha * accs[c][...] + _dot(p.astype(BF16), vb)
            out += [m_new, l]
        return tuple(out)

    init = (jnp.full((tq, 1), NEG_BIG, F32), jnp.zeros((tq, 1), F32)) * 2
    carry = lax.fori_loop(0, i, lambda j, c: step(j, c, False), init)
    _, l1, _, l2 = step(i, carry, True)
    o_ref[...] = _diff_combine(acc1_ref[...] / l1, acc2_ref[...] / l2, lam_ref, sg_ref, lambda_init)


def _flash_fast_body(lam_ref, sg_ref, q_ref, k_ref, vt_ref, o_ref, acc_ref, p_ref,
                     *, tq, heads, lambda_init):
    i = pl.program_id(2)
    lanes = lambda h: slice(h * LANES, (h + 1) * LANES)
    q2 = [jnp.concatenate(_split_components(q_ref[:, lanes(h)]), axis=0) for h in range(heads)]
    ones_rows = jnp.ones((ONES_ROWS, tq), BF16)
    key = lax.broadcasted_iota(jnp.int32, (tq, 2 * tq), 0)
    qry = lax.broadcasted_iota(jnp.int32, (tq, 2 * tq), 1)
    delta = key - jnp.where(qry >= tq, qry - tq, qry)

    def probs_t(h, j):
        p = jnp.exp2(_scores(k_ref[pl.ds(j * tq, tq), lanes(h)], q2[h]))
        return jnp.where(delta <= (i - j) * tq, p, 0.0).astype(BF16)

    def weighted_values_t(h, j):
        start = pl.multiple_of(j * tq, tq)
        vt = jnp.concatenate([vt_ref[h, :, pl.ds(start, tq)], ones_rows], axis=0)
        return _dot(vt, p_ref[h])

    for h in range(heads):
        p_ref[h] = probs_t(h, 0)
    acc_ref[...] = jnp.zeros_like(acc_ref)

    def body(j, carry):
        for h in range(heads):
            acc_ref[h] += weighted_values_t(h, j - 1)
            p_ref[h] = probs_t(h, j)
        return carry

    lax.fori_loop(1, i + 1, body, 0)
    lam = lam_ref[...]
    dot_sum = lambda a, b: jnp.sum(lam[a:a + 1] * lam[b:b + 1], axis=-1, keepdims=True)
    lam_full = jnp.exp(dot_sum(0, 1)) - jnp.exp(dot_sum(2, 3)) + lambda_init
    for h in range(heads):
        acc = acc_ref[h] + weighted_values_t(h, i)
        o = acc[:LANES] / acc[LANES:LANES + 1]
        o = o[:, :tq] - lam_full * o[:, tq:]
        o = o * lax.rsqrt(jnp.mean(o * o, axis=0, keepdims=True) + EPS) * sg_ref[...] * (1.0 - lambda_init)
        o_ref[:, lanes(h)] = o.T.astype(BF16)


def _flash_call(body, name, scratch, heads_per_step, transposed_v, q, k, v, lam, sub_gain,
                lambda_init, tq):
    b, l, d = q.shape
    width = heads_per_step * LANES
    qspec = pl.BlockSpec((None, tq, width), lambda bi, hi, i: (bi, i, hi))
    kspec = pl.BlockSpec((None, l, width), lambda bi, hi, i: (bi, 0, hi))
    vspec = kspec
    if transposed_v:
        vspec = pl.BlockSpec((None, heads_per_step, LANES, l), lambda bi, hi, i: (bi, hi, 0, 0))
    return pl.pallas_call(
        functools.partial(body, tq=tq, lambda_init=lambda_init),
        out_shape=jax.ShapeDtypeStruct((b, l, d), BF16),
        grid=(b, d // width, l // tq),
        in_specs=[_resident(lam.shape), _resident(sub_gain.shape), qspec, kspec, vspec],
        out_specs=qspec,
        scratch_shapes=scratch,
        compiler_params=_params("parallel", "parallel", "arbitrary"),
        name=name,
    )(lam, sub_gain, q, k, v)


def _flash_attention(q, k, v, vt, lam, sub_gain, score_bound, lambda_init, tq_fast=512, tq_safe=256, hps=4):
    tqf = min(tq_fast, q.shape[1])
    tqs = min(tq_safe, q.shape[1])
    sg = sub_gain.astype(F32)
    args = (q, k, v, vt, lam.astype(F32), sg)
    fast_scratch = [pltpu.VMEM((hps, LANES + ONES_ROWS, 2 * tqf), F32), pltpu.VMEM((hps, tqf, 2 * tqf), BF16)]
    safe_scratch = [pltpu.VMEM((tqs, LANES), F32)] * 2
    fast = lambda q, k, v, vt, lam, sg: _flash_call(
        functools.partial(_flash_fast_body, heads=hps), "flash_fast", fast_scratch, hps, True,
        q, k, vt, lam, sg[:, None], lambda_init, tqf)
    safe = lambda q, k, v, vt, lam, sg: _flash_call(
        _flash_safe_body, "flash_safe", safe_scratch, 1, False, q, k, v, lam, sg[None, :], lambda_init, tqs)
    return lax.cond(score_bound <= FAST_SCORE_BOUND, fast, safe, *args)


def _norm_proj_body(x_ref, g_ref, w_ref, o_ref):
    o_ref[...] = _dot(_rms(x_ref[...], g_ref[...]).astype(BF16), w_ref[...])


def _norm_proj(x, gain, w, tm=512):
    n, d = x.shape
    tm = min(tm, n)
    return pl.pallas_call(
        _norm_proj_body,
        out_shape=jax.ShapeDtypeStruct((n, w.shape[1]), F32),
        grid=(n // tm,),
        in_specs=[pl.BlockSpec((tm, d), lambda i: (i, 0)), _resident((1, d)), w.spec()],
        out_specs=pl.BlockSpec((tm, w.shape[1]), lambda i: (i, 0)),
        compiler_params=_params("parallel"),
        name="norm_proj",
    )(x, gain[None, :], w.stack)


def _relu2_mlp(x, g_ref, w1_ref, w2_ref, th):
    h = _rms(x, g_ref[...]).astype(BF16)
    acc = x
    for j in range(w1_ref.shape[1] // th):
        a = _dot(h, w1_ref[:, j * th:(j + 1) * th])
        a = jnp.square(jnp.maximum(a, 0.0)).astype(BF16)
        acc = acc + _dot(a, w2_ref[j * th:(j + 1) * th, :])
    return acc


def _mlp_body(x_ref, g_ref, w1_ref, w2_ref, o_ref, *, th):
    o_ref[...] = _relu2_mlp(x_ref[...], g_ref, w1_ref, w2_ref, th)


def _proj_mlp_body(a_ref, wp_ref, x_ref, g_ref, w1_ref, w2_ref, o_ref, *, th):
    x = x_ref[...] + _dot(a_ref[...], wp_ref[...])
    o_ref[...] = _relu2_mlp(x, g_ref, w1_ref, w2_ref, th)


def _glu_mlp_body(a_ref, wg_ref, bg_ref, wp_ref, x_ref, g_ref, w1_ref, w2_ref, o_ref, *, th):
    a = a_ref[...]
    gate = jax.nn.sigmoid(_dot(a, wg_ref[...]) + bg_ref[...])
    x = x_ref[...] + _dot((a.astype(F32) * gate).astype(BF16), wp_ref[...])
    o_ref[...] = _relu2_mlp(x, g_ref, w1_ref, w2_ref, th)


def _mlp(x, gain, w1, w2, proj=None, glu=None, tm=512, th=1024):
    n, d = x.shape
    tm = min(tm, n)
    row = pl.BlockSpec((tm, d), lambda i: (i, 0))
    pre, pre_specs, body, name = (), [], _mlp_body, "relu2_mlp"
    if proj is not None:
        a, wp = proj
        pre, body, name = (a, wp.stack), _proj_mlp_body, "proj_relu2_mlp"
        pre_specs = [pl.BlockSpec((tm, a.shape[1]), lambda i: (i, 0)), wp.spec()]
    elif glu is not None:
        a, wg, bg, wp = glu
        pre, body, name = (a, wg.stack, bg.astype(F32)[None, :], wp.stack), _glu_mlp_body, "glu_relu2_mlp"
        pre_specs = [row, wg.spec(), _resident((1, d)), wp.spec()]
    return pl.pallas_call(
        functools.partial(body, th=th),
        out_shape=jax.ShapeDtypeStruct((n, d), F32),
        grid=(n // tm,),
        in_specs=pre_specs + [row, _resident((1, d)), w1.spec(), w2.spec()],
        out_specs=row,
        compiler_params=_params("parallel"),
        name=name,
    )(*pre, x, gain[None, :], w1.stack, w2.stack)


def _cast_body(w_ref, o_ref):
    o_ref[...] = w_ref[...].astype(BF16)


def _to_bf16(w, tr=256):
    cols = w.shape[-1]
    w2 = w.reshape(-1, cols)
    blk = pl.BlockSpec((tr, cols), lambda i: (i, 0))
    out = pl.pallas_call(
        _cast_body,
        out_shape=jax.ShapeDtypeStruct(w2.shape, BF16),
        grid=(w2.shape[0] // tr,),
        in_specs=[blk],
        out_specs=blk,
        compiler_params=_params("parallel"),
        name="to_bf16",
    )(w2)
    return out.reshape(w.shape)


def _s5_build_operators(bd_ref, bb_ref, cb_ref, pw_ref, pwc_ref, m_ref, ws_ref, wo_ref):
    t = S5_CHUNK
    half = pw_ref.shape[2]
    m_ref[...] = jnp.zeros_like(m_ref)
    for s in range(t):
        for tt in range(s, t):
            m_ref[s * LANES:(s + 1) * LANES, tt * LANES:(tt + 1) * LANES] = bd_ref[tt - s]
    bbr, bbi = bb_ref[0], bb_ref[1]
    for s in range(t):
        pr = pw_ref[0, t - 1 - s:t - s, :]
        pi = pw_ref[1, t - 1 - s:t - s, :]
        ws_ref[s * LANES:(s + 1) * LANES, 0:half] = (bbr * pr - bbi * pi).astype(BF16)
        ws_ref[s * LANES:(s + 1) * LANES, half:2 * half] = (bbr * pi + bbi * pr).astype(BF16)
    cbr, cbi = cb_ref[0], cb_ref[1]
    for tt in range(t):
        pr = pwc_ref[0, :, tt + 1:tt + 2]
        pi = pwc_ref[1, :, tt + 1:tt + 2]
        wo_ref[0:half, tt * LANES:(tt + 1) * LANES] = (cbr * pr - cbi * pi).astype(BF16)
        wo_ref[half:2 * half, tt * LANES:(tt + 1) * LANES] = (-(cbr * pi + cbi * pr)).astype(BF16)


def _s5_scan_body(u_ref, bd_ref, bb_ref, cb_ref, pw_ref, pwc_ref, d_ref, o_ref,
                  m_ref, ws_ref, wo_ref, u2_ref, s_ref, xp_ref, y_ref, *, n_chunks):
    t = S5_CHUNK
    half = pw_ref.shape[2]

    @pl.when(pl.program_id(1) == 0)
    def _():
        _s5_build_operators(bd_ref, bb_ref, cb_ref, pw_ref, pwc_ref, m_ref, ws_ref, wo_ref)

    for s in range(t):
        u2_ref[:, s * LANES:(s + 1) * LANES] = u_ref[pl.ds(s, n_chunks, stride=t), :].astype(BF16)
    s_ref[...] = _dot(u2_ref[...], ws_ref[...])
    ar = pw_ref[0, t:t + 1, :]
    ai = pw_ref[1, t:t + 1, :]

    def carry_step(c, carry):
        xr, xi = carry
        xp_ref[pl.ds(c, 1), 0:half] = xr
        xp_ref[pl.ds(c, 1), half:2 * half] = xi
        sr = s_ref[pl.ds(c, 1), 0:half]
        si = s_ref[pl.ds(c, 1), half:2 * half]
        return ar * xr - ai * xi + sr, ar * xi + ai * xr + si

    zero = jnp.zeros((1, half), F32)
    lax.fori_loop(0, n_chunks, carry_step, (zero, zero))
    xp = xp_ref[...].astype(BF16)
    per = MXU_DIM // LANES
    for cb in range(t // per):
        cols = slice(cb * MXU_DIM, (cb + 1) * MXU_DIM)
        kdim = (cb + 1) * MXU_DIM
        y2 = _dot(u2_ref[:, 0:kdim], m_ref[0:kdim, cols]) + _dot(xp, wo_ref[:, cols])
        for r in range(per):
            y_ref[pl.ds(cb * per + r, n_chunks, stride=t), :] = y2[:, r * LANES:(r + 1) * LANES]
    y = y_ref[...] + d_ref[...] * u_ref[...]
    o_ref[...] = _gelu(y).astype(BF16)


def _s5_weights(a_re, a_im, b_re, b_im, c_re, c_im, log_dt):
    t = S5_CHUNK
    g, p = a_re.shape
    gl = LANES // S5_GROUP
    nj = g // gl
    hp = lax.Precision.HIGHEST
    lr = jnp.minimum(a_re.astype(F32), -1e-4)
    li = a_im.astype(F32)
    dt = jnp.exp(log_dt.astype(F32))[:, None]
    steps = jnp.arange(t + 1, dtype=F32)[:, None, None]
    mag = jnp.exp(steps * (lr * dt))
    pr = mag * jnp.cos(steps * (li * dt))
    pi = mag * jnp.sin(steps * (li * dt))
    nr, ni = pr[1] - 1.0, pi[1]
    den = lr * lr + li * li
    zr = ((nr * lr + ni * li) / den)[..., None]
    zi = ((ni * lr - nr * li) / den)[..., None]
    br, bi = b_re.astype(F32), b_im.astype(F32)
    bbr = zr * br - zi * bi
    bbi = zr * bi + zi * br
    cr, ci = c_re.astype(F32), c_im.astype(F32)
    wr = pr[:t, :, :, None] * bbr[None] - pi[:t, :, :, None] * bbi[None]
    wi = pr[:t, :, :, None] * bbi[None] + pi[:t, :, :, None] * bbr[None]
    kern = (jnp.einsum('ghp,kgpi->kgih', cr, wr, precision=hp)
            - jnp.einsum('ghp,kgpi->kgih', ci, wi, precision=hp))
    eye = jnp.eye(gl, dtype=F32)
    kern = kern.reshape(t, nj, gl, S5_GROUP, S5_GROUP)
    bd = jnp.einsum('kjgih,gq->jkgiqh', kern, eye).reshape(nj, t, LANES, LANES).astype(BF16)
    bb = jnp.stack([bbr, bbi]).reshape(2, nj, gl, p, S5_GROUP)
    bblk = jnp.einsum('rjgpi,gq->jrgiqp', bb, eye).reshape(nj, 2, LANES, gl * p)
    cc = jnp.stack([cr, ci]).reshape(2, nj, gl, S5_GROUP, p)
    cblk = jnp.einsum('rjghp,gq->jrgpqh', cc, eye).reshape(nj, 2, gl * p, LANES)
    pw = jnp.stack([pr, pi]).reshape(2, t + 1, nj, gl * p).transpose(2, 0, 1, 3)
    return bd, bblk, cblk, pw, pw.transpose(0, 1, 3, 2)


def _s5_scan(u, weights, d_skip):
    b, l, d = u.shape
    nj = weights[0].shape[0]
    n_chunks = l // S5_CHUNK
    tl = S5_CHUNK * LANES
    ns = 2 * weights[3].shape[3]
    col = pl.BlockSpec((None, l, LANES), lambda j, bi: (bi, 0, j))
    per_j = lambda w: pl.BlockSpec((None,) + w.shape[1:], lambda j, bi: (j,) + (0,) * (w.ndim - 1))
    return pl.pallas_call(
        functools.partial(_s5_scan_body, n_chunks=n_chunks),
        out_shape=jax.ShapeDtypeStruct((b, l, d), BF16),
        grid=(nj, b),
        in_specs=[col] + [per_j(w) for w in weights] + [pl.BlockSpec((1, LANES), lambda j, bi: (0, j))],
        out_specs=col,
        scratch_shapes=[pltpu.VMEM((tl, tl), BF16), pltpu.VMEM((tl, ns), BF16), pltpu.VMEM((ns, tl), BF16),
                        pltpu.VMEM((n_chunks, tl), BF16), pltpu.VMEM((n_chunks, ns), F32),
                        pltpu.VMEM((n_chunks, ns), F32), pltpu.VMEM((l, LANES), F32)],
        compiler_params=_params("arbitrary", "arbitrary"),
        name="s5_scan",
    )(u, *weights, d_skip.astype(F32)[None, :])


def _gmlp_body(x_ref, g_ref, win_ref, vg_ref, wsp_ref, bs_ref, wout_ref, o_ref, v_ref,
               *, half, heads):
    x = x_ref[...]
    tm = x.shape[0]
    hd = half // heads
    h = _rms(x, g_ref[...]).astype(BF16)
    per = 2
    wide = per * hd
    ss = jnp.zeros((tm, 1), F32)
    for c in range(heads // per):
        zv = _gelu(_dot(h, win_ref[:, half + c * wide:half + (c + 1) * wide]))
        ss = ss + jnp.sum(zv * zv, axis=-1, keepdims=True)
        v_ref[:, c * wide:(c + 1) * wide] = zv
    rinv = lax.rsqrt(ss * (1.0 / half) + EPS)
    row = lax.broadcasted_iota(jnp.int32, (GM_CHUNK, GM_CHUNK), 0)
    col = lax.broadcasted_iota(jnp.int32, (GM_CHUNK, GM_CHUNK), 1)

    def head_gate(head):
        sl = slice(head * hd, (head + 1) * hd)
        vh = (v_ref[:, sl] * rinv * vg_ref[:, sl]).astype(BF16)
        wm = jnp.where(col <= row, wsp_ref[head], 0.0).astype(BF16)
        bias = bs_ref[:, head:head + 1]
        return jnp.concatenate(
            [_dot(wm, vh[r * GM_CHUNK:(r + 1) * GM_CHUNK, :]) + bias for r in range(tm // GM_CHUNK)],
            axis=0)

    acc = x
    for c in range(heads // per):
        sl = slice(c * wide, (c + 1) * wide)
        zu = _gelu(_dot(h, win_ref[:, sl]))
        gate = jnp.concatenate([head_gate(c * per + r) for r in range(per)], axis=1)
        acc = acc + _dot((zu * gate).astype(BF16), wout_ref[sl, :])
    o_ref[...] = acc


def _gmlp(x, gain, w_in, v_gain, w_s, b_s, w_out, tm=256):
    n, d = x.shape
    tm = min(tm, n)
    half = w_out.shape[0]
    heads = w_s.shape[0]
    row = pl.BlockSpec((tm, d), lambda i: (i, 0))
    return pl.pallas_call(
        functools.partial(_gmlp_body, half=half, heads=heads),
        out_shape=jax.ShapeDtypeStruct((n, d), F32),
        grid=(n // tm,),
        in_specs=[row, _resident((1, d)), w_in.spec(), _resident((1, half)),
                  _resident(w_s.shape), _resident((GM_CHUNK, heads)), w_out.spec()],
        out_specs=row,
        scratch_shapes=[pltpu.VMEM((tm, half), F32)],
        compiler_params=_params("parallel"),
        name="gmlp",
    )(x, gain[None, :], w_in.stack, v_gain.astype(F32)[None, :], w_s.astype(F32), b_s.astype(F32).T,
      w_out.stack)


def kernel(x, positions, norm_mix, norm_mlp, mlp_w1, mlp_w2, attn_w_qkv, attn_q_norm, attn_k_norm, attn_lambda, attn_sub_norm, attn_w_o, ssm_w_in, ssm_a_re, ssm_a_im, ssm_b_re, ssm_b_im, ssm_c_re, ssm_c_im, ssm_d, ssm_log_dt, ssm_w_glu, ssm_b_glu, ssm_w_out, gm_w_in, gm_v_norm, gm_w_s, gm_b_s, gm_w_out):
    b, l, d = x.shape
    n = b * l
    depth = norm_mix.shape[0]
    xf = x.reshape(n, d)
    cos, sin = _rope_tables(positions)
    mlp_w1, mlp_w2, attn_w_qkv, attn_w_o = map(_to_bf16, (mlp_w1, mlp_w2, attn_w_qkv, attn_w_o))
    ssm_w_in, ssm_w_glu, ssm_w_out, gm_w_in, gm_w_out = map(
        _to_bf16, (ssm_w_in, ssm_w_glu, ssm_w_out, gm_w_in, gm_w_out))
    for i in range(depth):
        kind = i % N_MIXERS
        j = i // N_MIXERS
        mlp = functools.partial(_mlp, xf, norm_mlp[i], _Layer(mlp_w1, i), _Layer(mlp_w2, i))
        if kind == 0:
            lambda_init = 0.8 - 0.6 * math.exp(-0.3 * i)
            q, k, v, vt = _qkv_proj(xf, norm_mix[i], _Layer(attn_w_qkv, j), attn_q_norm[j], attn_k_norm[j],
                                    cos, sin, b)
            shape = (b, l, d)
            score_bound = (DA_HEAD_DIM ** 0.5 * jnp.max(jnp.abs(attn_q_norm[j]))
                           * jnp.max(jnp.abs(attn_k_norm[j])))
            o = _flash_attention(q.reshape(shape), k.reshape(shape), v.reshape(shape), vt,
                                 attn_lambda[j], attn_sub_norm[j], score_bound, lambda_init)
            xf = mlp(proj=(o.reshape(n, d), _Layer(attn_w_o, j)))
        elif kind == 1:
            u = _norm_proj(xf, norm_mix[i], _Layer(ssm_w_in, j))
            weights = _s5_weights(ssm_a_re[j], ssm_a_im[j], ssm_b_re[j], ssm_b_im[j],
                                  ssm_c_re[j], ssm_c_im[j], ssm_log_dt[j])
            g = _s5_scan(u.reshape(b, l, d), weights, ssm_d[j])
            xf = mlp(glu=(g.reshape(n, d), _Layer(ssm_w_glu, j), ssm_b_glu[j], _Layer(ssm_w_out, j)))
        else:
            xf = _gmlp(xf, norm_mix[i], _Layer(gm_w_in, j), gm_v_norm[j], gm_w_s[j], gm_b_s[j],
                       _Layer(gm_w_out, j))
            xf = _mlp(xf, norm_mlp[i], _Layer(mlp_w1, i), _Layer(mlp_w2, i))
    return xf.reshape(b, l, d)
```

```python
import functools
import math
from typing import NamedTuple

import jax
import jax.numpy as jnp
from jax import lax
from jax.experimental import pallas as pl
from jax.experimental.pallas import tpu as pltpu

F32 = jnp.float32
BF16 = jnp.bfloat16

EPS = 1e-6
ROPE_THETA = 10000.0
N_MIXERS = 3

DA_HEADS = 8
DA_HEAD_DIM = 64
S5_GROUP = 16
S5_STATE = 64
S5_CHUNK = 16
GM_HEADS = 8
GM_CHUNK = 128

LANES = 128
MXU_DIM = 256
ONES_ROWS = 16
VMEM_LIMIT = 56 * 1024 * 1024
NEG_BIG = -1e30
FAST_SCORE_BOUND = 30.0
LOG2E = math.log2(math.e)


def _params(*sem):
    return pltpu.CompilerParams(dimension_semantics=sem, vmem_limit_bytes=VMEM_LIMIT)


def _resident(shape):
    zeros = (0,) * len(shape)
    return pl.BlockSpec(shape, lambda *_: zeros, pipeline_mode=pl.Buffered(1))


class _Layer(NamedTuple):
    stack: jax.Array
    index: int

    @property
    def shape(self):
        return self.stack.shape[1:]

    def spec(self):
        idx = (self.index,) + (0,) * len(self.shape)
        return pl.BlockSpec((None,) + self.shape, lambda *_: idx, pipeline_mode=pl.Buffered(1))


def _rms(x, gain):
    return x * lax.rsqrt(jnp.mean(x * x, axis=-1, keepdims=True) + EPS) * gain


def _gelu(x):
    c = math.sqrt(2.0 / math.pi)
    return 0.5 * x * (1.0 + jnp.tanh(c * (x + 0.044715 * (x * x * x))))


def _dot(a, b):
    return lax.dot_general(a, b, (((1,), (0,)), ((), ())), preferred_element_type=F32)


def _rope_body(pos_ref, freq_ref, cos_ref, sin_ref):
    ang = pos_ref[...].astype(F32) * freq_ref[...]
    lane = lax.broadcasted_iota(jnp.int32, ang.shape, 1)
    s = jnp.sin(ang)
    cos_ref[...] = jnp.cos(ang)
    sin_ref[...] = jnp.where(lane % DA_HEAD_DIM < DA_HEAD_DIM // 2, -s, s)


def _rope_tables(positions):
    n = positions.size
    tm = min(n, 2048)
    inv_freq = ROPE_THETA ** (-jnp.arange(0, DA_HEAD_DIM, 2, dtype=F32) / DA_HEAD_DIM)
    freq = jnp.tile(inv_freq, LANES // (DA_HEAD_DIM // 2))[None, :]
    return pl.pallas_call(
        _rope_body,
        out_shape=(jax.ShapeDtypeStruct((n, LANES), F32),) * 2,
        grid=(n // tm,),
        in_specs=[pl.BlockSpec((tm, 1), lambda i: (i, 0)), _resident((1, LANES))],
        out_specs=(pl.BlockSpec((tm, LANES), lambda i: (i, 0)),) * 2,
        compiler_params=_params("parallel"),
        name="rope_tables",
    )(positions.reshape(n, 1), freq)


def _qkv_body(x_ref, g_ref, w_ref, qg_ref, kg_ref, cos_ref, sin_ref, bd_ref,
              q_ref, k_ref, v_ref, vt_ref, *, d):
    h = _rms(x_ref[...], g_ref[...]).astype(BF16)
    cos = cos_ref[...]
    sin = sin_ref[...]
    lane = lax.broadcasted_iota(jnp.int32, cos.shape, 1)
    first_half = lane % DA_HEAD_DIM < DA_HEAD_DIM // 2
    bd = bd_ref[...]
    seg = bd.shape[0]

    def norm_rope(col0, gain_ref, out_ref, scale):
        acc = _dot(h, w_ref[:, col0:col0 + d])
        sq = (acc * acc).astype(BF16)
        gain = gain_ref[...]
        for c in range(d // seg):
            ss = _dot(sq[:, c * seg:(c + 1) * seg], bd)
            r = lax.rsqrt(ss * (1.0 / DA_HEAD_DIM) + EPS)
            for hh in range(seg // LANES):
                lo = c * seg + hh * LANES
                qn = acc[:, lo:lo + LANES] * r[:, hh * LANES:(hh + 1) * LANES] * gain
                partner = jnp.where(first_half,
                                    pltpu.roll(qn, LANES - DA_HEAD_DIM // 2, 1),
                                    pltpu.roll(qn, DA_HEAD_DIM // 2, 1))
                out = qn * cos + partner * sin
                out_ref[:, lo:lo + LANES] = (out * scale).astype(BF16)

    norm_rope(0, qg_ref, q_ref, DA_HEAD_DIM ** -0.5 * LOG2E)
    norm_rope(d, kg_ref, k_ref, 1.0)
    v = _dot(h, w_ref[:, 2 * d:3 * d])
    v_ref[...] = v.astype(BF16)
    for hh in range(d // LANES):
        vt_ref[hh] = v[:, hh * LANES:(hh + 1) * LANES].T.astype(BF16)


def _qkv_proj(x, gain, w_qkv, q_gain, k_gain, cos, sin, batch, tm=512):
    n, d = x.shape
    tm = min(tm, n // batch)
    tiles = n // batch // tm
    seg = MXU_DIM
    r = jnp.arange(seg) // DA_HEAD_DIM
    bd = (r[:, None] == r[None, :]).astype(BF16)
    tile = lambda g: jnp.tile(g.astype(F32), LANES // DA_HEAD_DIM)[None, :]
    row = pl.BlockSpec((tm, d), lambda i: (i, 0))
    tab = pl.BlockSpec((tm, LANES), lambda i: (i, 0))
    return pl.pallas_call(
        functools.partial(_qkv_body, d=d),
        out_shape=(jax.ShapeDtypeStruct((n, d), BF16),) * 3
        + (jax.ShapeDtypeStruct((batch, d // LANES, LANES, n // batch), BF16),),
        grid=(n // tm,),
        in_specs=[row, _resident((1, d)), w_qkv.spec(), _resident((1, LANES)),
                  _resident((1, LANES)), tab, tab, _resident((seg, seg))],
        out_specs=(row,) * 3
        + (pl.BlockSpec((None, d // LANES, LANES, tm), lambda i: (i // tiles, 0, 0, i % tiles)),),
        compiler_params=_params("parallel"),
        name="qkv_proj",
    )(x, gain[None, :], w_qkv.stack, tile(q_gain), tile(k_gain), cos, sin, bd)


def _split_components(q):
    lane = lax.broadcasted_iota(jnp.int32, q.shape, 1)
    zero = jnp.zeros_like(q)
    return jnp.where(lane < DA_HEAD_DIM, q, zero), jnp.where(lane >= DA_HEAD_DIM, q, zero)


def _scores(qc, kb):
    return lax.dot_general(qc, kb, (((1,), (1,)), ((), ())), preferred_element_type=F32)


def _causal(shape):
    row = lax.broadcasted_iota(jnp.int32, shape, 0)
    col = lax.broadcasted_iota(jnp.int32, shape, 1)
    return col <= row


def _diff_combine(o1, o2, lam_ref, sg_ref, lambda_init):
    lam = lam_ref[...]
    dot_sum = lambda a, b: jnp.sum(lam[a:a + 1] * lam[b:b + 1], axis=-1, keepdims=True)
    lam_full = jnp.exp(dot_sum(0, 1)) - jnp.exp(dot_sum(2, 3)) + lambda_init
    o = o1 - lam_full * o2
    return (_rms(o, sg_ref[...]) * (1.0 - lambda_init)).astype(BF16)


def _flash_safe_body(lam_ref, sg_ref, q_ref, k_ref, v_ref, o_ref, acc1_ref, acc2_ref,
                     *, tq, lambda_init):
    i = pl.program_id(2)
    qs = _split_components(q_ref[...])
    accs = (acc1_ref, acc2_ref)
    acc1_ref[...] = jnp.zeros_like(acc1_ref)
    acc2_ref[...] = jnp.zeros_like(acc2_ref)

    def step(j, carry, masked):
        kb = k_ref[pl.ds(j * tq, tq), :]
        vb = v_ref[pl.ds(j * tq, tq), :]
        out = []
        for c in range(2):
            m, l = carry[2 * c], carry[2 * c + 1]
            s = _scores(qs[c], kb)
            if masked:
                s = jnp.where(_causal(s.shape), s, NEG_BIG)
            m_new = jnp.maximum(m, jnp.max(s, axis=-1, keepdims=True))
            alpha = jnp.exp2(m - m_new)
            p = jnp.exp2(s - m_new)
            l = alpha * l + jnp.sum(p, axis=-1, keepdims=True)
            accs[c][...] = alpha * accs[c][...] + _dot(p.astype(BF16), vb)
            out += [m_new, l]
        return tuple(out)

    init = (jnp.full((tq, 1), NEG_BIG, F32), jnp.zeros((tq, 1), F32)) * 2
    carry = lax.fori_loop(0, i, lambda j, c: step(j, c, False), init)
    _, l1, _, l2 = step(i, carry, True)
    o_ref[...] = _diff_combine(acc1_ref[...] / l1, acc2_ref[...] / l2, lam_ref, sg_ref, lambda_init)


def _flash_fast_body(lam_ref, sg_ref, q_ref, k_ref, vt_ref, o_ref, acc_ref, p_ref,
                     *, tq, heads, lambda_init):
    i = pl.program_id(2)
    lanes = lambda h: slice(h * LANES, (h + 1) * LANES)
    q2 = [jnp.concatenate(_split_components(q_ref[:, lanes(h)]), axis=0) for h in range(heads)]
    ones_rows = jnp.ones((ONES_ROWS, tq), BF16)
    key = lax.broadcasted_iota(jnp.int32, (tq, 2 * tq), 0)
    qry = lax.broadcasted_iota(jnp.int32, (tq, 2 * tq), 1)
    delta = key - jnp.where(qry >= tq, qry - tq, qry)

    def probs_t(h, j):
        p = jnp.exp2(_scores(k_ref[pl.ds(j * tq, tq), lanes(h)], q2[h]))
        return jnp.where(delta <= (i - j) * tq, p, 0.0).astype(BF16)

    def weighted_values_t(h, j):
        start = pl.multiple_of(j * tq, tq)
        vt = jnp.concatenate([vt_ref[h, :, pl.ds(start, tq)], ones_rows], axis=0)
        return _dot(vt, p_ref[h])

    for h in range(heads):
        p_ref[h] = probs_t(h, 0)
    acc_ref[...] = jnp.zeros_like(acc_ref)

    def body(j, carry):
        for h in range(heads):
            acc_ref[h] += weighted_values_t(h, j - 1)
            p_ref[h] = probs_t(h, j)
        return carry

    lax.fori_loop(1, i + 1, body, 0)
    lam = lam_ref[...]
    dot_sum = lambda a, b: jnp.sum(lam[a:a + 1] * lam[b:b + 1], axis=-1, keepdims=True)
    lam_full = jnp.exp(dot_sum(0, 1)) - jnp.exp(dot_sum(2, 3)) + lambda_init
    for h in range(heads):
        acc = acc_ref[h] + weighted_values_t(h, i)
        o = acc[:LANES] / acc[LANES:LANES + 1]
        o = o[:, :tq] - lam_full * o[:, tq:]
        o = o * lax.rsqrt(jnp.mean(o * o, axis=0, keepdims=True) + EPS) * sg_ref[...] * (1.0 - lambda_init)
        o_ref[:, lanes(h)] = o.T.astype(BF16)


def _flash_call(body, name, scratch, heads_per_step, transposed_v, q, k, v, lam, sub_gain,
                lambda_init, tq):
    b, l, d = q.shape
    width = heads_per_step * LANES
    qspec = pl.BlockSpec((None, tq, width), lambda bi, hi, i: (bi, i, hi))
    kspec = pl.BlockSpec((None, l, width), lambda bi, hi, i: (bi, 0, hi))
    vspec = kspec
    if transposed_v:
        vspec = pl.BlockSpec((None, heads_per_step, LANES, l), lambda bi, hi, i: (bi, hi, 0, 0))
    return pl.pallas_call(
        functools.partial(body, tq=tq, lambda_init=lambda_init),
        out_shape=jax.ShapeDtypeStruct((b, l, d), BF16),
        grid=(b, d // width, l // tq),
        in_specs=[_resident(lam.shape), _resident(sub_gain.shape), qspec, kspec, vspec],
        out_specs=qspec,
        scratch_shapes=scratch,
        compiler_params=_params("parallel", "parallel", "arbitrary"),
        name=name,
    )(lam, sub_gain, q, k, v)


def _flash_attention(q, k, v, vt, lam, sub_gain, score_bound, lambda_init, tq_fast=512, tq_safe=256, hps=4):
    tqf = min(tq_fast, q.shape[1])
    tqs = min(tq_safe, q.shape[1])
    sg = sub_gain.astype(F32)
    args = (q, k, v, vt, lam.astype(F32), sg)
    fast_scratch = [pltpu.VMEM((hps, LANES + ONES_ROWS, 2 * tqf), F32), pltpu.VMEM((hps, tqf, 2 * tqf), BF16)]
    safe_scratch = [pltpu.VMEM((tqs, LANES), F32)] * 2
    fast = lambda q, k, v, vt, lam, sg: _flash_call(
        functools.partial(_flash_fast_body, heads=hps), "flash_fast", fast_scratch, hps, True,
        q, k, vt, lam, sg[:, None], lambda_init, tqf)
    safe = lambda q, k, v, vt, lam, sg: _flash_call(
        _flash_safe_body, "flash_safe", safe_scratch, 1, False, q, k, v, lam, sg[None, :], lambda_init, tqs)
    return lax.cond(score_bound <= FAST_SCORE_BOUND, fast, safe, *args)


def _norm_proj_body(x_ref, g_ref, w_ref, o_ref):
    o_ref[...] = _dot(_rms(x_ref[...], g_ref[...]).astype(BF16), w_ref[...])


def _norm_proj(x, gain, w, tm=512):
    n, d = x.shape
    tm = min(tm, n)
    return pl.pallas_call(
        _norm_proj_body,
        out_shape=jax.ShapeDtypeStruct((n, w.shape[1]), F32),
        grid=(n // tm,),
        in_specs=[pl.BlockSpec((tm, d), lambda i: (i, 0)), _resident((1, d)), w.spec()],
        out_specs=pl.BlockSpec((tm, w.shape[1]), lambda i: (i, 0)),
        compiler_params=_params("parallel"),
        name="norm_proj",
    )(x, gain[None, :], w.stack)


def _relu2_mlp(x, g_ref, w1_ref, w2_ref, th):
    h = _rms(x, g_ref[...]).astype(BF16)
    acc = x
    for j in range(w1_ref.shape[1] // th):
        a = _dot(h, w1_ref[:, j * th:(j + 1) * th])
        a = jnp.square(jnp.maximum(a, 0.0)).astype(BF16)
        acc = acc + _dot(a, w2_ref[j * th:(j + 1) * th, :])
    return acc


def _mlp_body(x_ref, g_ref, w1_ref, w2_ref, o_ref, *, th):
    o_ref[...] = _relu2_mlp(x_ref[...], g_ref, w1_ref, w2_ref, th)


def _proj_mlp_body(a_ref, wp_ref, x_ref, g_ref, w1_ref, w2_ref, o_ref, *, th):
    x = x_ref[...] + _dot(a_ref[...], wp_ref[...])
    o_ref[...] = _relu2_mlp(x, g_ref, w1_ref, w2_ref, th)


def _glu_mlp_body(a_ref, wg_ref, bg_ref, wp_ref, x_ref, g_ref, w1_ref, w2_ref, o_ref, *, th):
    a = a_ref[...]
    gate = jax.nn.sigmoid(_dot(a, wg_ref[...]) + bg_ref[...])
    x = x_ref[...] + _dot((a.astype(F32) * gate).astype(BF16), wp_ref[...])
    o_ref[...] = _relu2_mlp(x, g_ref, w1_ref, w2_ref, th)


def _mlp(x, gain, w1, w2, proj=None, glu=None, tm=512, th=1024):
    n, d = x.shape
    tm = min(tm, n)
    row = pl.BlockSpec((tm, d), lambda i: (i, 0))
    pre, pre_specs, body, name = (), [], _mlp_body, "relu2_mlp"
    if proj is not None:
        a, wp = proj
        pre, body, name = (a, wp.stack), _proj_mlp_body, "proj_relu2_mlp"
        pre_specs = [pl.BlockSpec((tm, a.shape[1]), lambda i: (i, 0)), wp.spec()]
    elif glu is not None:
        a, wg, bg, wp = glu
        pre, body, name = (a, wg.stack, bg.astype(F32)[None, :], wp.stack), _glu_mlp_body, "glu_relu2_mlp"
        pre_specs = [row, wg.spec(), _resident((1, d)), wp.spec()]
    return pl.pallas_call(
        functools.partial(body, th=th),
        out_shape=jax.ShapeDtypeStruct((n, d), F32),
        grid=(n // tm,),
        in_specs=pre_specs + [row, _resident((1, d)), w1.spec(), w2.spec()],
        out_specs=row,
        compiler_params=_params("parallel"),
        name=name,
    )(*pre, x, gain[None, :], w1.stack, w2.stack)


def _s5_build_operators(bd_ref, bb_ref, cb_ref, pw_ref, pwc_ref, m_ref, ws_ref, wo_ref):
    t = S5_CHUNK
    half = pw_ref.shape[2]
    m_ref[...] = jnp.zeros_like(m_ref)
    for s in range(t):
        for tt in range(s, t):
            m_ref[s * LANES:(s + 1) * LANES, tt * LANES:(tt + 1) * LANES] = bd_ref[tt - s]
    bbr, bbi = bb_ref[0], bb_ref[1]
    for s in range(t):
        pr = pw_ref[0, t - 1 - s:t - s, :]
        pi = pw_ref[1, t - 1 - s:t - s, :]
        ws_ref[s * LANES:(s + 1) * LANES, 0:half] = (bbr * pr - bbi * pi).astype(BF16)
        ws_ref[s * LANES:(s + 1) * LANES, half:2 * half] = (bbr * pi + bbi * pr).astype(BF16)
    cbr, cbi = cb_ref[0], cb_ref[1]
    for tt in range(t):
        pr = pwc_ref[0, :, tt + 1:tt + 2]
        pi = pwc_ref[1, :, tt + 1:tt + 2]
        wo_ref[0:half, tt * LANES:(tt + 1) * LANES] = (cbr * pr - cbi * pi).astype(BF16)
        wo_ref[half:2 * half, tt * LANES:(tt + 1) * LANES] = (-(cbr * pi + cbi * pr)).astype(BF16)


def _s5_scan_body(u_ref, bd_ref, bb_ref, cb_ref, pw_ref, pwc_ref, d_ref, o_ref,
                  m_ref, ws_ref, wo_ref, u2_ref, s_ref, xp_ref, y_ref, *, n_chunks):
    t = S5_CHUNK
    half = pw_ref.shape[2]

    @pl.when(pl.program_id(1) == 0)
    def _():
        _s5_build_operators(bd_ref, bb_ref, cb_ref, pw_ref, pwc_ref, m_ref, ws_ref, wo_ref)

    for s in range(t):
        u2_ref[:, s * LANES:(s + 1) * LANES] = u_ref[pl.ds(s, n_chunks, stride=t), :].astype(BF16)
    s_ref[...] = _dot(u2_ref[...], ws_ref[...])
    ar = pw_ref[0, t:t + 1, :]
    ai = pw_ref[1, t:t + 1, :]

    def carry_step(c, carry):
        xr, xi = carry
        xp_ref[pl.ds(c, 1), 0:half] = xr
        xp_ref[pl.ds(c, 1), half:2 * half] = xi
        sr = s_ref[pl.ds(c, 1), 0:half]
        si = s_ref[pl.ds(c, 1), half:2 * half]
        return ar * xr - ai * xi + sr, ar * xi + ai * xr + si

    zero = jnp.zeros((1, half), F32)
    lax.fori_loop(0, n_chunks, carry_step, (zero, zero))
    xp = xp_ref[...].astype(BF16)
    per = MXU_DIM // LANES
    for cb in range(t // per):
        cols = slice(cb * MXU_DIM, (cb + 1) * MXU_DIM)
        kdim = (cb + 1) * MXU_DIM
        y2 = _dot(u2_ref[:, 0:kdim], m_ref[0:kdim, cols]) + _dot(xp, wo_ref[:, cols])
        for r in range(per):
            y_ref[pl.ds(cb * per + r, n_chunks, stride=t), :] = y2[:, r * LANES:(r + 1) * LANES]
    y = y_ref[...] + d_ref[...] * u_ref[...]
    o_ref[...] = _gelu(y).astype(BF16)


def _s5_weights(a_re, a_im, b_re, b_im, c_re, c_im, log_dt):
    t = S5_CHUNK
    g, p = a_re.shape
    gl = LANES // S5_GROUP
    nj = g // gl
    hp = lax.Precision.HIGHEST
    lr = jnp.minimum(a_re.astype(F32), -1e-4)
    li = a_im.astype(F32)
    dt = jnp.exp(log_dt.astype(F32))[:, None]
    steps = jnp.arange(t + 1, dtype=F32)[:, None, None]
    mag = jnp.exp(steps * (lr * dt))
    pr = mag * jnp.cos(steps * (li * dt))
    pi = mag * jnp.sin(steps * (li * dt))
    nr, ni = pr[1] - 1.0, pi[1]
    den = lr * lr + li * li
    zr = ((nr * lr + ni * li) / den)[..., None]
    zi = ((ni * lr - nr * li) / den)[..., None]
    br, bi = b_re.astype(F32), b_im.astype(F32)
    bbr = zr * br - zi * bi
    bbi = zr * bi + zi * br
    cr, ci = c_re.astype(F32), c_im.astype(F32)
    wr = pr[:t, :, :, None] * bbr[None] - pi[:t, :, :, None] * bbi[None]
    wi = pr[:t, :, :, None] * bbi[None] + pi[:t, :, :, None] * bbr[None]
    kern = (jnp.einsum('ghp,kgpi->kgih', cr, wr, precision=hp)
            - jnp.einsum('ghp,kgpi->kgih', ci, wi, precision=hp))
    eye = jnp.eye(gl, dtype=F32)
    kern = kern.reshape(t, nj, gl, S5_GROUP, S5_GROUP)
    bd = jnp.einsum('kjgih,gq->jkgiqh', kern, eye).reshape(nj, t, LANES, LANES).astype(BF16)
    bb = jnp.stack([bbr, bbi]).reshape(2, nj, gl, p, S5_GROUP)
    bblk = jnp.einsum('rjgpi,gq->jrgiqp', bb, eye).reshape(nj, 2, LANES, gl * p)
    cc = jnp.stack([cr, ci]).reshape(2, nj, gl, S5_GROUP, p)
    cblk = jnp.einsum('rjghp,gq->jrgpqh', cc, eye).reshape(nj, 2, gl * p, LANES)
    pw = jnp.stack([pr, pi]).reshape(2, t + 1, nj, gl * p).transpose(2, 0, 1, 3)
    return bd, bblk, cblk, pw, pw.transpose(0, 1, 3, 2)


def _s5_scan(u, weights, d_skip):
    b, l, d = u.shape
    nj = weights[0].shape[0]
    n_chunks = l // S5_CHUNK
    tl = S5_CHUNK * LANES
    ns = 2 * weights[3].shape[3]
    col = pl.BlockSpec((None, l, LANES), lambda j, bi: (bi, 0, j))
    per_j = lambda w: pl.BlockSpec((None,) + w.shape[1:], lambda j, bi: (j,) + (0,) * (w.ndim - 1))
    return pl.pallas_call(
        functools.partial(_s5_scan_body, n_chunks=n_chunks),
        out_shape=jax.ShapeDtypeStruct((b, l, d), BF16),
        grid=(nj, b),
        in_specs=[col] + [per_j(w) for w in weights] + [pl.BlockSpec((1, LANES), lambda j, bi: (0, j))],
        out_specs=col,
        scratch_shapes=[pltpu.VMEM((tl, tl), BF16), pltpu.VMEM((tl, ns), BF16), pltpu.VMEM((ns, tl), BF16),
                        pltpu.VMEM((n_chunks, tl), BF16), pltpu.VMEM((n_chunks, ns), F32),
                        pltpu.VMEM((n_chunks, ns), F32), pltpu.VMEM((l, LANES), F32)],
        compiler_params=_params("arbitrary", "arbitrary"),
        name="s5_scan",
    )(u, *weights, d_skip.astype(F32)[None, :])


def _gmlp_body(x_ref, g_ref, win_ref, vg_ref, wsp_ref, bs_ref, wout_ref, o_ref, v_ref,
               *, half, heads):
    x = x_ref[...]
    tm = x.shape[0]
    hd = half // heads
    h = _rms(x, g_ref[...]).astype(BF16)
    per = 2
    wide = per * hd
    ss = jnp.zeros((tm, 1), F32)
    for c in range(heads // per):
        zv = _gelu(_dot(h, win_ref[:, half + c * wide:half + (c + 1) * wide]))
        ss = ss + jnp.sum(zv * zv, axis=-1, keepdims=True)
        v_ref[:, c * wide:(c + 1) * wide] = zv
    rinv = lax.rsqrt(ss * (1.0 / half) + EPS)
    row = lax.broadcasted_iota(jnp.int32, (GM_CHUNK, GM_CHUNK), 0)
    col = lax.broadcasted_iota(jnp.int32, (GM_CHUNK, GM_CHUNK), 1)

    def head_gate(head):
        sl = slice(head * hd, (head + 1) * hd)
        vh = (v_ref[:, sl] * rinv * vg_ref[:, sl]).astype(BF16)
        wm = jnp.where(col <= row, wsp_ref[head], 0.0).astype(BF16)
        bias = bs_ref[:, head:head + 1]
        return jnp.concatenate(
            [_dot(wm, vh[r * GM_CHUNK:(r + 1) * GM_CHUNK, :]) + bias for r in range(tm // GM_CHUNK)],
            axis=0)

    acc = x
    for c in range(heads // per):
        sl = slice(c * wide, (c + 1) * wide)
        zu = _gelu(_dot(h, win_ref[:, sl]))
        gate = jnp.concatenate([head_gate(c * per + r) for r in range(per)], axis=1)
        acc = acc + _dot((zu * gate).astype(BF16), wout_ref[sl, :])
    o_ref[...] = acc


def _gmlp(x, gain, w_in, v_gain, w_s, b_s, w_out, tm=256):
    n, d = x.shape
    tm = min(tm, n)
    half = w_out.shape[0]
    heads = w_s.shape[0]
    row = pl.BlockSpec((tm, d), lambda i: (i, 0))
    return pl.pallas_call(
        functools.partial(_gmlp_body, half=half, heads=heads),
        out_shape=jax.ShapeDtypeStruct((n, d), F32),
        grid=(n // tm,),
        in_specs=[row, _resident((1, d)), w_in.spec(), _resident((1, half)),
                  _resident(w_s.shape), _resident((GM_CHUNK, heads)), w_out.spec()],
        out_specs=row,
        scratch_shapes=[pltpu.VMEM((tm, half), F32)],
        compiler_params=_params("parallel"),
        name="gmlp",
    )(x, gain[None, :], w_in.stack, v_gain.astype(F32)[None, :], w_s.astype(F32), b_s.astype(F32).T,
      w_out.stack)


def kernel(x, positions, norm_mix, norm_mlp, mlp_w1, mlp_w2, attn_w_qkv, attn_q_norm, attn_k_norm, attn_lambda, attn_sub_norm, attn_w_o, ssm_w_in, ssm_a_re, ssm_a_im, ssm_b_re, ssm_b_im, ssm_c_re, ssm_c_im, ssm_d, ssm_log_dt, ssm_w_glu, ssm_b_glu, ssm_w_out, gm_w_in, gm_v_norm, gm_w_s, gm_b_s, gm_w_out):
    b, l, d = x.shape
    n = b * l
    depth = norm_mix.shape[0]
    xf = x.reshape(n, d)
    cos, sin = _rope_tables(positions)
    for i in range(depth):
        kind = i % N_MIXERS
        j = i // N_MIXERS
        mlp = functools.partial(_mlp, xf, norm_mlp[i], _Layer(mlp_w1, i), _Layer(mlp_w2, i))
        if kind == 0:
            lambda_init = 0.8 - 0.6 * math.exp(-0.3 * i)
            q, k, v, vt = _qkv_proj(xf, norm_mix[i], _Layer(attn_w_qkv, j), attn_q_norm[j], attn_k_norm[j],
                                    cos, sin, b)
            shape = (b, l, d)
            score_bound = (DA_HEAD_DIM ** 0.5 * jnp.max(jnp.abs(attn_q_norm[j]))
                           * jnp.max(jnp.abs(attn_k_norm[j])))
            o = _flash_attention(q.reshape(shape), k.reshape(shape), v.reshape(shape), vt,
                                 attn_lambda[j], attn_sub_norm[j], score_bound, lambda_init)
            xf = mlp(proj=(o.reshape(n, d), _Layer(attn_w_o, j)))
        elif kind == 1:
            u = _norm_proj(xf, norm_mix[i], _Layer(ssm_w_in, j))
            weights = _s5_weights(ssm_a_re[j], ssm_a_im[j], ssm_b_re[j], ssm_b_im[j],
                                  ssm_c_re[j], ssm_c_im[j], ssm_log_dt[j])
            g = _s5_scan(u.reshape(b, l, d), weights, ssm_d[j])
            xf = mlp(glu=(g.reshape(n, d), _Layer(ssm_w_glu, j), ssm_b_glu[j], _Layer(ssm_w_out, j)))
        else:
            xf = _gmlp(xf, norm_mix[i], _Layer(gm_w_in, j), gm_v_norm[j], gm_w_s[j], gm_b_s[j],
                       _Layer(gm_w_out, j))
            xf = _mlp(xf, norm_mlp[i], _Layer(mlp_w1, i), _Layer(mlp_w2, i))
    return xf.reshape(b, l, d)
```

```python
import functools
import math
from typing import NamedTuple

import jax
import jax.numpy as jnp
from jax import lax
from jax.experimental import pallas as pl
from jax.experimental.pallas import tpu as pltpu

F32 = jnp.float32
BF16 = jnp.bfloat16

EPS = 1e-6
ROPE_THETA = 10000.0
N_MIXERS = 3

DA_HEADS = 8
DA_HEAD_DIM = 64
S5_GROUP = 16
S5_STATE = 64
S5_CHUNK = 16
GM_HEADS = 8
GM_CHUNK = 128

LANES = 128
MXU_DIM = 256
ONES_ROWS = 16
VMEM_LIMIT = 56 * 1024 * 1024
NEG_BIG = -1e30
FAST_SCORE_BOUND = 30.0
LOG2E = math.log2(math.e)


def _params(*sem):
    return pltpu.CompilerParams(dimension_semantics=sem, vmem_limit_bytes=VMEM_LIMIT)


def _resident(shape):
    zeros = (0,) * len(shape)
    return pl.BlockSpec(shape, lambda *_: zeros, pipeline_mode=pl.Buffered(1))


class _Layer(NamedTuple):
    stack: jax.Array
    index: int

    @property
    def shape(self):
        return self.stack.shape[1:]

    def spec(self):
        idx = (self.index,) + (0,) * len(self.shape)
        return pl.BlockSpec((None,) + self.shape, lambda *_: idx, pipeline_mode=pl.Buffered(1))


def _rms(x, gain):
    return x * lax.rsqrt(jnp.mean(x * x, axis=-1, keepdims=True) + EPS) * gain


def _gelu(x):
    c = math.sqrt(2.0 / math.pi)
    return 0.5 * x * (1.0 + jnp.tanh(c * (x + 0.044715 * (x * x * x))))


def _dot(a, b):
    return lax.dot_general(a, b, (((1,), (0,)), ((), ())), preferred_element_type=F32)


def _rope_body(pos_ref, freq_ref, cos_ref, sin_ref):
    ang = pos_ref[...].astype(F32) * freq_ref[...]
    lane = lax.broadcasted_iota(jnp.int32, ang.shape, 1)
    s = jnp.sin(ang)
    cos_ref[...] = jnp.cos(ang)
    sin_ref[...] = jnp.where(lane % DA_HEAD_DIM < DA_HEAD_DIM // 2, -s, s)


def _rope_tables(positions):
    n = positions.size
    tm = min(n, 2048)
    inv_freq = ROPE_THETA ** (-jnp.arange(0, DA_HEAD_DIM, 2, dtype=F32) / DA_HEAD_DIM)
    freq = jnp.tile(inv_freq, LANES // (DA_HEAD_DIM // 2))[None, :]
    return pl.pallas_call(
        _rope_body,
        out_shape=(jax.ShapeDtypeStruct((n, LANES), F32),) * 2,
        grid=(n // tm,),
        in_specs=[pl.BlockSpec((tm, 1), lambda i: (i, 0)), _resident((1, LANES))],
        out_specs=(pl.BlockSpec((tm, LANES), lambda i: (i, 0)),) * 2,
        compiler_params=_params("parallel"),
        name="rope_tables",
    )(positions.reshape(n, 1), freq)


def _qkv_body(x_ref, g_ref, w_ref, qg_ref, kg_ref, cos_ref, sin_ref, bd_ref,
              q_ref, k_ref, v_ref, vt_ref, *, d):
    h = _rms(x_ref[...], g_ref[...]).astype(BF16)
    cos = cos_ref[...]
    sin = sin_ref[...]
    lane = lax.broadcasted_iota(jnp.int32, cos.shape, 1)
    first_half = lane % DA_HEAD_DIM < DA_HEAD_DIM // 2
    bd = bd_ref[...]
    seg = bd.shape[0]

    def norm_rope(col0, gain_ref, out_ref, scale):
        acc = _dot(h, w_ref[:, col0:col0 + d])
        sq = (acc * acc).astype(BF16)
        gain = gain_ref[...]
        for c in range(d // seg):
            ss = _dot(sq[:, c * seg:(c + 1) * seg], bd)
            r = lax.rsqrt(ss * (1.0 / DA_HEAD_DIM) + EPS)
            for hh in range(seg // LANES):
                lo = c * seg + hh * LANES
                qn = acc[:, lo:lo + LANES] * r[:, hh * LANES:(hh + 1) * LANES] * gain
                partner = jnp.where(first_half,
                                    pltpu.roll(qn, LANES - DA_HEAD_DIM // 2, 1),
                                    pltpu.roll(qn, DA_HEAD_DIM // 2, 1))
                out = qn * cos + partner * sin
                out_ref[:, lo:lo + LANES] = (out * scale).astype(BF16)

    norm_rope(0, qg_ref, q_ref, DA_HEAD_DIM ** -0.5 * LOG2E)
    norm_rope(d, kg_ref, k_ref, 1.0)
    v = _dot(h, w_ref[:, 2 * d:3 * d])
    v_ref[...] = v.astype(BF16)
    for hh in range(d // LANES):
        vt_ref[hh] = v[:, hh * LANES:(hh + 1) * LANES].T.astype(BF16)


def _qkv_proj(x, gain, w_qkv, q_gain, k_gain, cos, sin, batch, tm=512):
    n, d = x.shape
    tm = min(tm, n // batch)
    tiles = n // batch // tm
    seg = MXU_DIM
    r = jnp.arange(seg) // DA_HEAD_DIM
    bd = (r[:, None] == r[None, :]).astype(BF16)
    tile = lambda g: jnp.tile(g.astype(F32), LANES // DA_HEAD_DIM)[None, :]
    row = pl.BlockSpec((tm, d), lambda i: (i, 0))
    tab = pl.BlockSpec((tm, LANES), lambda i: (i, 0))
    return pl.pallas_call(
        functools.partial(_qkv_body, d=d),
        out_shape=(jax.ShapeDtypeStruct((n, d), BF16),) * 3
        + (jax.ShapeDtypeStruct((batch, d // LANES, LANES, n // batch), BF16),),
        grid=(n // tm,),
        in_specs=[row, _resident((1, d)), w_qkv.spec(), _resident((1, LANES)),
                  _resident((1, LANES)), tab, tab, _resident((seg, seg))],
        out_specs=(row,) * 3
        + (pl.BlockSpec((None, d // LANES, LANES, tm), lambda i: (i // tiles, 0, 0, i % tiles)),),
        compiler_params=_params("parallel"),
        name="qkv_proj",
    )(x, gain[None, :], w_qkv.stack, tile(q_gain), tile(k_gain), cos, sin, bd)


def _split_components(q):
    lane = lax.broadcasted_iota(jnp.int32, q.shape, 1)
    zero = jnp.zeros_like(q)
    return jnp.where(lane < DA_HEAD_DIM, q, zero), jnp.where(lane >= DA_HEAD_DIM, q, zero)


def _scores(qc, kb):
    return lax.dot_general(qc, kb, (((1,), (1,)), ((), ())), preferred_element_type=F32)


def _causal(shape):
    row = lax.broadcasted_iota(jnp.int32, shape, 0)
    col = lax.broadcasted_iota(jnp.int32, shape, 1)
    return col <= row


def _diff_combine(o1, o2, lam_ref, sg_ref, lambda_init):
    lam = lam_ref[...]
    dot_sum = lambda a, b: jnp.sum(lam[a:a + 1] * lam[b:b + 1], axis=-1, keepdims=True)
    lam_full = jnp.exp(dot_sum(0, 1)) - jnp.exp(dot_sum(2, 3)) + lambda_init
    o = o1 - lam_full * o2
    return (_rms(o, sg_ref[...]) * (1.0 - lambda_init)).astype(BF16)


def _flash_safe_body(lam_ref, sg_ref, q_ref, k_ref, v_ref, o_ref, acc1_ref, acc2_ref,
                     *, tq, lambda_init):
    i = pl.program_id(2)
    qs = _split_components(q_ref[...])
    accs = (acc1_ref, acc2_ref)
    acc1_ref[...] = jnp.zeros_like(acc1_ref)
    acc2_ref[...] = jnp.zeros_like(acc2_ref)

    def step(j, carry, masked):
        kb = k_ref[pl.ds(j * tq, tq), :]
        vb = v_ref[pl.ds(j * tq, tq), :]
        out = []
        for c in range(2):
            m, l = carry[2 * c], carry[2 * c + 1]
            s = _scores(qs[c], kb)
            if masked:
                s = jnp.where(_causal(s.shape), s, NEG_BIG)
            m_new = jnp.maximum(m, jnp.max(s, axis=-1, keepdims=True))
            alpha = jnp.exp2(m - m_new)
            p = jnp.exp2(s - m_new)
            l = alpha * l + jnp.sum(p, axis=-1, keepdims=True)
            accs[c][...] = alpha * accs[c][...] + _dot(p.astype(BF16), vb)
            out += [m_new, l]
        return tuple(out)

    init = (jnp.full((tq, 1), NEG_BIG, F32), jnp.zeros((tq, 1), F32)) * 2
    carry = lax.fori_loop(0, i, lambda j, c: step(j, c, False), init)
    _, l1, _, l2 = step(i, carry, True)
    o_ref[...] = _diff_combine(acc1_ref[...] / l1, acc2_ref[...] / l2, lam_ref, sg_ref, lambda_init)


def _flash_fast_body(lam_ref, sg_ref, q_ref, k_ref, vt_ref, o_ref, acc_ref, p_ref,
                     *, tq, heads, lambda_init):
    i = pl.program_id(2)
    lanes = lambda h: slice(h * LANES, (h + 1) * LANES)
    q2 = [jnp.concatenate(_split_components(q_ref[:, lanes(h)]), axis=0) for h in range(heads)]
    ones_rows = jnp.ones((ONES_ROWS, tq), BF16)
    key = lax.broadcasted_iota(jnp.int32, (tq, 2 * tq), 0)
    qry = lax.broadcasted_iota(jnp.int32, (tq, 2 * tq), 1)
    delta = key - jnp.where(qry >= tq, qry - tq, qry)

    def probs_t(h, j):
        p = jnp.exp2(_scores(k_ref[pl.ds(j * tq, tq), lanes(h)], q2[h]))
        return jnp.where(delta <= (i - j) * tq, p, 0.0).astype(BF16)

    def weighted_values_t(h, j):
        start = pl.multiple_of(j * tq, tq)
        vt = jnp.concatenate([vt_ref[h, :, pl.ds(start, tq)], ones_rows], axis=0)
        return _dot(vt, p_ref[h])

    for h in range(heads):
        p_ref[h] = probs_t(h, 0)
    acc_ref[...] = jnp.zeros_like(acc_ref)

    def body(j, carry):
        for h in range(heads):
            acc_ref[h] += weighted_values_t(h, j - 1)
            p_ref[h] = probs_t(h, j)
        return carry

    lax.fori_loop(1, i + 1, body, 0)
    lam = lam_ref[...]
    dot_sum = lambda a, b: jnp.sum(lam[a:a + 1] * lam[b:b + 1], axis=-1, keepdims=True)
    lam_full = jnp.exp(dot_sum(0, 1)) - jnp.exp(dot_sum(2, 3)) + lambda_init
    for h in range(heads):
        acc = acc_ref[h] + weighted_values_t(h, i)
        o = acc[:LANES] / acc[LANES:LANES + 1]
        o = o[:, :tq] - lam_full * o[:, tq:]
        o = o * lax.rsqrt(jnp.mean(o * o, axis=0, keepdims=True) + EPS) * sg_ref[...] * (1.0 - lambda_init)
        o_ref[:, lanes(h)] = o.T.astype(BF16)


def _flash_call(body, name, scratch, heads_per_step, transposed_v, q, k, v, lam, sub_gain,
                lambda_init, tq):
    b, l, d = q.shape
    width = heads_per_step * LANES
    qspec = pl.BlockSpec((None, tq, width), lambda bi, hi, i: (bi, i, hi))
    kspec = pl.BlockSpec((None, l, width), lambda bi, hi, i: (bi, 0, hi))
    vspec = kspec
    if transposed_v:
        vspec = pl.BlockSpec((None, heads_per_step, LANES, l), lambda bi, hi, i: (bi, hi, 0, 0))
    return pl.pallas_call(
        functools.partial(body, tq=tq, lambda_init=lambda_init),
        out_shape=jax.ShapeDtypeStruct((b, l, d), BF16),
        grid=(b, d // width, l // tq),
        in_specs=[_resident(lam.shape), _resident(sub_gain.shape), qspec, kspec, vspec],
        out_specs=qspec,
        scratch_shapes=scratch,
        compiler_params=_params("parallel", "parallel", "arbitrary"),
        name=name,
    )(lam, sub_gain, q, k, v)


def _flash_attention(q, k, v, vt, lam, sub_gain, score_bound, lambda_init, tq_fast=512, tq_safe=256, hps=4):
    tqf = min(tq_fast, q.shape[1])
    tqs = min(tq_safe, q.shape[1])
    sg = sub_gain.astype(F32)
    args = (q, k, v, vt, lam.astype(F32), sg)
    fast_scratch = [pltpu.VMEM((hps, LANES + ONES_ROWS, 2 * tqf), F32), pltpu.VMEM((hps, tqf, 2 * tqf), BF16)]
    safe_scratch = [pltpu.VMEM((tqs, LANES), F32)] * 2
    fast = lambda q, k, v, vt, lam, sg: _flash_call(
        functools.partial(_flash_fast_body, heads=hps), "flash_fast", fast_scratch, hps, True,
        q, k, vt, lam, sg[:, None], lambda_init, tqf)
    safe = lambda q, k, v, vt, lam, sg: _flash_call(
        _flash_safe_body, "flash_safe", safe_scratch, 1, False, q, k, v, lam, sg[None, :], lambda_init, tqs)
    return lax.cond(score_bound <= FAST_SCORE_BOUND, fast, safe, *args)


def _norm_proj_body(x_ref, g_ref, w_ref, o_ref):
    o_ref[...] = _dot(_rms(x_ref[...], g_ref[...]).astype(BF16), w_ref[...])


def _norm_proj(x, gain, w, tm=512):
    n, d = x.shape
    tm = min(tm, n)
    return pl.pallas_call(
        _norm_proj_body,
        out_shape=jax.ShapeDtypeStruct((n, w.shape[1]), F32),
        grid=(n // tm,),
        in_specs=[pl.BlockSpec((tm, d), lambda i: (i, 0)), _resident((1, d)), w.spec()],
        out_specs=pl.BlockSpec((tm, w.shape[1]), lambda i: (i, 0)),
        compiler_params=_params("parallel"),
        name="norm_proj",
    )(x, gain[None, :], w.stack)


def _relu2_mlp(x, g_ref, w1_ref, w2_ref, th):
    h = _rms(x, g_ref[...]).astype(BF16)
    acc = x
    for j in range(w1_ref.shape[1] // th):
        a = _dot(h, w1_ref[:, j * th:(j + 1) * th])
        a = jnp.square(jnp.maximum(a, 0.0)).astype(BF16)
        acc = acc + _dot(a, w2_ref[j * th:(j + 1) * th, :])
    return acc


def _mlp_body(x_ref, g_ref, w1_ref, w2_ref, o_ref, *, th):
    o_ref[...] = _relu2_mlp(x_ref[...], g_ref, w1_ref, w2_ref, th)


def _proj_mlp_body(a_ref, wp_ref, x_ref, g_ref, w1_ref, w2_ref, o_ref, *, th):
    x = x_ref[...] + _dot(a_ref[...], wp_ref[...])
    o_ref[...] = _relu2_mlp(x, g_ref, w1_ref, w2_ref, th)


def _glu_mlp_body(a_ref, wg_ref, bg_ref, wp_ref, x_ref, g_ref, w1_ref, w2_ref, o_ref, *, th):
    a = a_ref[...]
    gate = jax.nn.sigmoid(_dot(a, wg_ref[...]) + bg_ref[...])
    x = x_ref[...] + _dot((a.astype(F32) * gate).astype(BF16), wp_ref[...])
    o_ref[...] = _relu2_mlp(x, g_ref, w1_ref, w2_ref, th)


def _mlp(x, gain, w1, w2, proj=None, glu=None, tm=512, th=1024):
    n, d = x.shape
    tm = min(tm, n)
    row = pl.BlockSpec((tm, d), lambda i: (i, 0))
    pre, pre_specs, body, name = (), [], _mlp_body, "relu2_mlp"
    if proj is not None:
        a, wp = proj
        pre, body, name = (a, wp.stack), _proj_mlp_body, "proj_relu2_mlp"
        pre_specs = [pl.BlockSpec((tm, a.shape[1]), lambda i: (i, 0)), wp.spec()]
    elif glu is not None:
        a, wg, bg, wp = glu
        pre, body, name = (a, wg.stack, bg.astype(F32)[None, :], wp.stack), _glu_mlp_body, "glu_relu2_mlp"
        pre_specs = [row, wg.spec(), _resident((1, d)), wp.spec()]
    return pl.pallas_call(
        functools.partial(body, th=th),
        out_shape=jax.ShapeDtypeStruct((n, d), F32),
        grid=(n // tm,),
        in_specs=pre_specs + [row, _resident((1, d)), w1.spec(), w2.spec()],
        out_specs=row,
        compiler_params=_params("parallel"),
        name=name,
    )(*pre, x, gain[None, :], w1.stack, w2.stack)


def _s5_build_operators(bd_ref, bb_ref, cb_ref, pw_ref, pwc_ref, m_ref, ws_ref, wo_ref):
    t = S5_CHUNK
    half = pw_ref.shape[2]
    m_ref[...] = jnp.zeros_like(m_ref)
    for s in range(t):
        for tt in range(s, t):
            m_ref[s * LANES:(s + 1) * LANES, tt * LANES:(tt + 1) * LANES] = bd_ref[tt - s]
    bbr, bbi = bb_ref[0], bb_ref[1]
    for s in range(t):
        pr = pw_ref[0, t - 1 - s:t - s, :]
        pi = pw_ref[1, t - 1 - s:t - s, :]
        ws_ref[s * LANES:(s + 1) * LANES, 0:half] = (bbr * pr - bbi * pi).astype(BF16)
        ws_ref[s * LANES:(s + 1) * LANES, half:2 * half] = (bbr * pi + bbi * pr).astype(BF16)
    cbr, cbi = cb_ref[0], cb_ref[1]
    for tt in range(t):
        pr = pwc_ref[0, :, tt + 1:tt + 2]
        pi = pwc_ref[1, :, tt + 1:tt + 2]
        wo_ref[0:half, tt * LANES:(tt + 1) * LANES] = (cbr * pr - cbi * pi).astype(BF16)
        wo_ref[half:2 * half, tt * LANES:(tt + 1) * LANES] = (-(cbr * pi + cbi * pr)).astype(BF16)


def _s5_scan_body(u_ref, bd_ref, bb_ref, cb_ref, pw_ref, pwc_ref, d_ref, o_ref,
                  m_ref, ws_ref, wo_ref, u2_ref, s_ref, xp_ref, y_ref, *, n_chunks):
    t = S5_CHUNK
    half = pw_ref.shape[2]

    @pl.when(pl.program_id(1) == 0)
    def _():
        _s5_build_operators(bd_ref, bb_ref, cb_ref, pw_ref, pwc_ref, m_ref, ws_ref, wo_ref)

    for s in range(t):
        u2_ref[:, s * LANES:(s + 1) * LANES] = u_ref[pl.ds(s, n_chunks, stride=t), :].astype(BF16)
    s_ref[...] = _dot(u2_ref[...], ws_ref[...])
    ar = pw_ref[0, t:t + 1, :]
    ai = pw_ref[1, t:t + 1, :]

    def carry_step(c, carry):
        xr, xi = carry
        xp_ref[pl.ds(c, 1), 0:half] = xr
        xp_ref[pl.ds(c, 1), half:2 * half] = xi
        sr = s_ref[pl.ds(c, 1), 0:half]
        si = s_ref[pl.ds(c, 1), half:2 * half]
        return ar * xr - ai * xi + sr, ar * xi + ai * xr + si

    zero = jnp.zeros((1, half), F32)
    lax.fori_loop(0, n_chunks, carry_step, (zero, zero))
    xp = xp_ref[...].astype(BF16)
    per = MXU_DIM // LANES
    for cb in range(t // per):
        cols = slice(cb * MXU_DIM, (cb + 1) * MXU_DIM)
        kdim = (cb + 1) * MXU_DIM
        y2 = _dot(u2_ref[:, 0:kdim], m_ref[0:kdim, cols]) + _dot(xp, wo_ref[:, cols])
        for r in range(per):
            y_ref[pl.ds(cb * per + r, n_chunks, stride=t), :] = y2[:, r * LANES:(r + 1) * LANES]
    y = y_ref[...] + d_ref[...] * u_ref[...]
    o_ref[...] = _gelu(y).astype(BF16)


def _s5_weights(a_re, a_im, b_re, b_im, c_re, c_im, log_dt):
    t = S5_CHUNK
    g, p = a_re.shape
    gl = LANES // S5_GROUP
    nj = g // gl
    hp = lax.Precision.HIGHEST
    lr = jnp.minimum(a_re.astype(F32), -1e-4)
    li = a_im.astype(F32)
    dt = jnp.exp(log_dt.astype(F32))[:, None]
    steps = jnp.arange(t + 1, dtype=F32)[:, None, None]
    mag = jnp.exp(steps * (lr * dt))
    pr = mag * jnp.cos(steps * (li * dt))
    pi = mag * jnp.sin(steps * (li * dt))
    nr, ni = pr[1] - 1.0, pi[1]
    den = lr * lr + li * li
    zr = ((nr * lr + ni * li) / den)[..., None]
    zi = ((ni * lr - nr * li) / den)[..., None]
    br, bi = b_re.astype(F32), b_im.astype(F32)
    bbr = zr * br - zi * bi
    bbi = zr * bi + zi * br
    cr, ci = c_re.astype(F32), c_im.astype(F32)
    wr = pr[:t, :, :, None] * bbr[None] - pi[:t, :, :, None] * bbi[None]
    wi = pr[:t, :, :, None] * bbi[None] + pi[:t, :, :, None] * bbr[None]
    kern = (jnp.einsum('ghp,kgpi->kgih', cr, wr, precision=hp)
            - jnp.einsum('ghp,kgpi->kgih', ci, wi, precision=hp))
    def group_diagonal(w, rows_per_group, cols_per_group):
        w = jnp.tile(w, (1,) * (w.ndim - 1) + (gl,))
        rg = jnp.arange(w.shape[-2])[:, None] // rows_per_group
        cg = jnp.arange(w.shape[-1])[None, :] // cols_per_group
        return jnp.where(rg == cg, w, 0.0)

    kern = kern.reshape(t, nj, gl * S5_GROUP, S5_GROUP)
    bd = group_diagonal(kern, S5_GROUP, S5_GROUP).transpose(1, 0, 2, 3).astype(BF16)
    bb = jnp.stack([bbr, bbi]).transpose(0, 1, 3, 2).reshape(2, nj, gl * S5_GROUP, p)
    bblk = group_diagonal(bb, S5_GROUP, p).transpose(1, 0, 2, 3)
    cc = jnp.stack([cr, ci]).transpose(0, 1, 3, 2).reshape(2, nj, gl * p, S5_GROUP)
    cblk = group_diagonal(cc, p, S5_GROUP).transpose(1, 0, 2, 3)
    pw = jnp.stack([pr, pi]).reshape(2, t + 1, nj, gl * p).transpose(2, 0, 1, 3)
    return bd, bblk, cblk, pw, pw.transpose(0, 1, 3, 2)


def _s5_scan(u, weights, d_skip):
    b, l, d = u.shape
    nj = weights[0].shape[0]
    n_chunks = l // S5_CHUNK
    tl = S5_CHUNK * LANES
    ns = 2 * weights[3].shape[3]
    col = pl.BlockSpec((None, l, LANES), lambda j, bi: (bi, 0, j))
    per_j = lambda w: pl.BlockSpec((None,) + w.shape[1:], lambda j, bi: (j,) + (0,) * (w.ndim - 1))
    return pl.pallas_call(
        functools.partial(_s5_scan_body, n_chunks=n_chunks),
        out_shape=jax.ShapeDtypeStruct((b, l, d), BF16),
        grid=(nj, b),
        in_specs=[col] + [per_j(w) for w in weights] + [pl.BlockSpec((1, LANES), lambda j, bi: (0, j))],
        out_specs=col,
        scratch_shapes=[pltpu.VMEM((tl, tl), BF16), pltpu.VMEM((tl, ns), BF16), pltpu.VMEM((ns, tl), BF16),
                        pltpu.VMEM((n_chunks, tl), BF16), pltpu.VMEM((n_chunks, ns), F32),
                        pltpu.VMEM((n_chunks, ns), F32), pltpu.VMEM((l, LANES), F32)],
        compiler_params=_params("arbitrary", "arbitrary"),
        name="s5_scan",
    )(u, *weights, d_skip.astype(F32)[None, :])


def _gmlp_body(x_ref, g_ref, win_ref, vg_ref, wsp_ref, bs_ref, wout_ref, o_ref, v_ref,
               *, half, heads):
    x = x_ref[...]
    tm = x.shape[0]
    hd = half // heads
    h = _rms(x, g_ref[...]).astype(BF16)
    per = 2
    wide = per * hd
    ss = jnp.zeros((tm, 1), F32)
    for c in range(heads // per):
        zv = _gelu(_dot(h, win_ref[:, half + c * wide:half + (c + 1) * wide]))
        ss = ss + jnp.sum(zv * zv, axis=-1, keepdims=True)
        v_ref[:, c * wide:(c + 1) * wide] = zv
    rinv = lax.rsqrt(ss * (1.0 / half) + EPS)
    row = lax.broadcasted_iota(jnp.int32, (GM_CHUNK, GM_CHUNK), 0)
    col = lax.broadcasted_iota(jnp.int32, (GM_CHUNK, GM_CHUNK), 1)

    def head_gate(head):
        sl = slice(head * hd, (head + 1) * hd)
        vh = (v_ref[:, sl] * rinv * vg_ref[:, sl]).astype(BF16)
        wm = jnp.where(col <= row, wsp_ref[head], 0.0).astype(BF16)
        bias = bs_ref[:, head:head + 1]
        return jnp.concatenate(
            [_dot(wm, vh[r * GM_CHUNK:(r + 1) * GM_CHUNK, :]) + bias for r in range(tm // GM_CHUNK)],
            axis=0)

    acc = x
    for c in range(heads // per):
        sl = slice(c * wide, (c + 1) * wide)
        zu = _gelu(_dot(h, win_ref[:, sl]))
        gate = jnp.concatenate([head_gate(c * per + r) for r in range(per)], axis=1)
        acc = acc + _dot((zu * gate).astype(BF16), wout_ref[sl, :])
    o_ref[...] = acc


def _gmlp(x, gain, w_in, v_gain, w_s, b_s, w_out, tm=256):
    n, d = x.shape
    tm = min(tm, n)
    half = w_out.shape[0]
    heads = w_s.shape[0]
    row = pl.BlockSpec((tm, d), lambda i: (i, 0))
    return pl.pallas_call(
        functools.partial(_gmlp_body, half=half, heads=heads),
        out_shape=jax.ShapeDtypeStruct((n, d), F32),
        grid=(n // tm,),
        in_specs=[row, _resident((1, d)), w_in.spec(), _resident((1, half)),
                  _resident(w_s.shape), _resident((GM_CHUNK, heads)), w_out.spec()],
        out_specs=row,
        scratch_shapes=[pltpu.VMEM((tm, half), F32)],
        compiler_params=_params("parallel"),
        name="gmlp",
    )(x, gain[None, :], w_in.stack, v_gain.astype(F32)[None, :], w_s.astype(F32), b_s.astype(F32).T,
      w_out.stack)


def kernel(x, positions, norm_mix, norm_mlp, mlp_w1, mlp_w2, attn_w_qkv, attn_q_norm, attn_k_norm, attn_lambda, attn_sub_norm, attn_w_o, ssm_w_in, ssm_a_re, ssm_a_im, ssm_b_re, ssm_b_im, ssm_c_re, ssm_c_im, ssm_d, ssm_log_dt, ssm_w_glu, ssm_b_glu, ssm_w_out, gm_w_in, gm_v_norm, gm_w_s, gm_b_s, gm_w_out):
    b, l, d = x.shape
    n = b * l
    depth = norm_mix.shape[0]
    xf = x.reshape(n, d)
    cos, sin = _rope_tables(positions)
    for i in range(depth):
        kind = i % N_MIXERS
        j = i // N_MIXERS
        mlp = functools.partial(_mlp, xf, norm_mlp[i], _Layer(mlp_w1, i), _Layer(mlp_w2, i))
        if kind == 0:
            lambda_init = 0.8 - 0.6 * math.exp(-0.3 * i)
            q, k, v, vt = _qkv_proj(xf, norm_mix[i], _Layer(attn_w_qkv, j), attn_q_norm[j], attn_k_norm[j],
                                    cos, sin, b)
            shape = (b, l, d)
            score_bound = (DA_HEAD_DIM ** 0.5 * jnp.max(jnp.abs(attn_q_norm[j]))
                           * jnp.max(jnp.abs(attn_k_norm[j])))
            o = _flash_attention(q.reshape(shape), k.reshape(shape), v.reshape(shape), vt,
                                 attn_lambda[j], attn_sub_norm[j], score_bound, lambda_init)
            xf = mlp(proj=(o.reshape(n, d), _Layer(attn_w_o, j)))
        elif kind == 1:
            u = _norm_proj(xf, norm_mix[i], _Layer(ssm_w_in, j))
            weights = _s5_weights(ssm_a_re[j], ssm_a_im[j], ssm_b_re[j], ssm_b_im[j],
                                  ssm_c_re[j], ssm_c_im[j], ssm_log_dt[j])
            g = _s5_scan(u.reshape(b, l, d), weights, ssm_d[j])
            xf = mlp(glu=(g.reshape(n, d), _Layer(ssm_w_glu, j), ssm_b_glu[j], _Layer(ssm_w_out, j)))
        else:
            xf = _gmlp(xf, norm_mix[i], _Layer(gm_w_in, j), gm_v_norm[j], gm_w_s[j], gm_b_s[j],
                       _Layer(gm_w_out, j))
            xf = _mlp(xf, norm_mlp[i], _Layer(mlp_w1, i), _Layer(mlp_w2, i))
    return xf.reshape(b, l, d)
```

```python
import functools
import math
from typing import NamedTuple

import jax
import jax.numpy as jnp
from jax import lax
from jax.experimental import pallas as pl
from jax.experimental.pallas import tpu as pltpu

F32 = jnp.float32
BF16 = jnp.bfloat16

EPS = 1e-6
ROPE_THETA = 10000.0
N_MIXERS = 3

DA_HEAD_DIM = 64
S5_GROUP = 16
S5_CHUNK = 16
GM_CHUNK = 128

LANES = 128
MXU_DIM = 256
ONES_ROWS = 16
VMEM_LIMIT = 56 * 1024 * 1024
NEG_BIG = -1e30
FAST_SCORE_BOUND = 30.0
LOG2E = math.log2(math.e)


def _params(*sem):
    return pltpu.CompilerParams(dimension_semantics=sem, vmem_limit_bytes=VMEM_LIMIT)


def _resident(shape):
    zeros = (0,) * len(shape)
    return pl.BlockSpec(shape, lambda *_: zeros, pipeline_mode=pl.Buffered(1))


class _Layer(NamedTuple):
    stack: jax.Array
    index: int

    @property
    def shape(self):
        return self.stack.shape[1:]

    def spec(self):
        idx = (self.index,) + (0,) * len(self.shape)
        return pl.BlockSpec((None,) + self.shape, lambda *_: idx, pipeline_mode=pl.Buffered(1))


def _rms(x, gain):
    return x * lax.rsqrt(jnp.mean(x * x, axis=-1, keepdims=True) + EPS) * gain


def _gelu(x):
    c = math.sqrt(2.0 / math.pi)
    return 0.5 * x * (1.0 + jnp.tanh(c * (x + 0.044715 * (x * x * x))))


def _dot(a, b):
    return lax.dot_general(a, b, (((1,), (0,)), ((), ())), preferred_element_type=F32)


def _rope_body(pos_ref, freq_ref, cos_ref, sin_ref):
    ang = pos_ref[...].astype(F32) * freq_ref[...]
    lane = lax.broadcasted_iota(jnp.int32, ang.shape, 1)
    s = jnp.sin(ang)
    cos_ref[...] = jnp.cos(ang)
    sin_ref[...] = jnp.where(lane % DA_HEAD_DIM < DA_HEAD_DIM // 2, -s, s)


def _rope_tables(positions):
    n = positions.size
    tm = min(n, 2048)
    inv_freq = ROPE_THETA ** (-jnp.arange(0, DA_HEAD_DIM, 2, dtype=F32) / DA_HEAD_DIM)
    freq = jnp.tile(inv_freq, LANES // (DA_HEAD_DIM // 2))[None, :]
    return pl.pallas_call(
        _rope_body,
        out_shape=(jax.ShapeDtypeStruct((n, LANES), F32),) * 2,
        grid=(n // tm,),
        in_specs=[pl.BlockSpec((tm, 1), lambda i: (i, 0)), _resident((1, LANES))],
        out_specs=(pl.BlockSpec((tm, LANES), lambda i: (i, 0)),) * 2,
        compiler_params=_params("parallel"),
        name="rope_tables",
    )(positions.reshape(n, 1), freq)


def _qkv_body(x_ref, g_ref, w_ref, qg_ref, kg_ref, cos_ref, sin_ref, bd_ref,
              q_ref, k_ref, v_ref, vt_ref, *, d):
    h = _rms(x_ref[...], g_ref[...]).astype(BF16)
    cos = cos_ref[...]
    sin = sin_ref[...]
    lane = lax.broadcasted_iota(jnp.int32, cos.shape, 1)
    first_half = lane % DA_HEAD_DIM < DA_HEAD_DIM // 2
    bd = bd_ref[...]
    seg = bd.shape[0]

    def norm_rope(col0, gain_ref, out_ref, scale):
        acc = _dot(h, w_ref[:, col0:col0 + d])
        sq = (acc * acc).astype(BF16)
        gain = gain_ref[...]
        for c in range(d // seg):
            ss = _dot(sq[:, c * seg:(c + 1) * seg], bd)
            r = lax.rsqrt(ss * (1.0 / DA_HEAD_DIM) + EPS)
            for hh in range(seg // LANES):
                lo = c * seg + hh * LANES
                qn = acc[:, lo:lo + LANES] * r[:, hh * LANES:(hh + 1) * LANES] * gain
                partner = jnp.where(first_half,
                                    pltpu.roll(qn, LANES - DA_HEAD_DIM // 2, 1),
                                    pltpu.roll(qn, DA_HEAD_DIM // 2, 1))
                out = qn * cos + partner * sin
                out_ref[:, lo:lo + LANES] = (out * scale).astype(BF16)

    norm_rope(0, qg_ref, q_ref, DA_HEAD_DIM ** -0.5 * LOG2E)
    norm_rope(d, kg_ref, k_ref, 1.0)
    v = _dot(h, w_ref[:, 2 * d:3 * d])
    v_ref[...] = v.astype(BF16)
    for hh in range(d // LANES):
        vt_ref[hh] = v[:, hh * LANES:(hh + 1) * LANES].T.astype(BF16)


def _qkv_proj(x, gain, w_qkv, q_gain, k_gain, cos, sin, batch, tm=512):
    n, d = x.shape
    tm = min(tm, n // batch)
    tiles = n // batch // tm
    seg = MXU_DIM
    r = jnp.arange(seg) // DA_HEAD_DIM
    bd = (r[:, None] == r[None, :]).astype(BF16)
    tile = lambda g: jnp.tile(g.astype(F32), LANES // DA_HEAD_DIM)[None, :]
    row = pl.BlockSpec((tm, d), lambda i: (i, 0))
    tab = pl.BlockSpec((tm, LANES), lambda i: (i, 0))
    return pl.pallas_call(
        functools.partial(_qkv_body, d=d),
        out_shape=(jax.ShapeDtypeStruct((n, d), BF16),) * 3
        + (jax.ShapeDtypeStruct((batch, d // LANES, LANES, n // batch), BF16),),
        grid=(n // tm,),
        in_specs=[row, _resident((1, d)), w_qkv.spec(), _resident((1, LANES)),
                  _resident((1, LANES)), tab, tab, _resident((seg, seg))],
        out_specs=(row,) * 3
        + (pl.BlockSpec((None, d // LANES, LANES, tm), lambda i: (i // tiles, 0, 0, i % tiles)),),
        compiler_params=_params("parallel"),
        name="qkv_proj",
    )(x, gain[None, :], w_qkv.stack, tile(q_gain), tile(k_gain), cos, sin, bd)


def _split_components(q):
    lane = lax.broadcasted_iota(jnp.int32, q.shape, 1)
    zero = jnp.zeros_like(q)
    return jnp.where(lane < DA_HEAD_DIM, q, zero), jnp.where(lane >= DA_HEAD_DIM, q, zero)


def _scores(qc, kb):
    return lax.dot_general(qc, kb, (((1,), (1,)), ((), ())), preferred_element_type=F32)


def _causal(shape):
    row = lax.broadcasted_iota(jnp.int32, shape, 0)
    col = lax.broadcasted_iota(jnp.int32, shape, 1)
    return col <= row


def _diff_combine(o1, o2, lam_ref, sg_ref, lambda_init):
    lam = lam_ref[...]
    dot_sum = lambda a, b: jnp.sum(lam[a:a + 1] * lam[b:b + 1], axis=-1, keepdims=True)
    lam_full = jnp.exp(dot_sum(0, 1)) - jnp.exp(dot_sum(2, 3)) + lambda_init
    o = o1 - lam_full * o2
    return (_rms(o, sg_ref[...]) * (1.0 - lambda_init)).astype(BF16)


def _flash_safe_body(lam_ref, sg_ref, q_ref, k_ref, v_ref, o_ref, acc1_ref, acc2_ref,
                     *, tq, lambda_init):
    i = pl.program_id(2)
    qs = _split_components(q_ref[...])
    accs = (acc1_ref, acc2_ref)
    acc1_ref[...] = jnp.zeros_like(acc1_ref)
    acc2_ref[...] = jnp.zeros_like(acc2_ref)

    def step(j, carry, masked):
        kb = k_ref[pl.ds(j * tq, tq), :]
        vb = v_ref[pl.ds(j * tq, tq), :]
        out = []
        for c in range(2):
            m, l = carry[2 * c], carry[2 * c + 1]
            s = _scores(qs[c], kb)
            if masked:
                s = jnp.where(_causal(s.shape), s, NEG_BIG)
            m_new = jnp.maximum(m, jnp.max(s, axis=-1, keepdims=True))
            alpha = jnp.exp2(m - m_new)
            p = jnp.exp2(s - m_new)
            l = alpha * l + jnp.sum(p, axis=-1, keepdims=True)
            accs[c][...] = alpha * accs[c][...] + _dot(p.astype(BF16), vb)
            out += [m_new, l]
        return tuple(out)

    init = (jnp.full((tq, 1), NEG_BIG, F32), jnp.zeros((tq, 1), F32)) * 2
    carry = lax.fori_loop(0, i, lambda j, c: step(j, c, False), init)
    _, l1, _, l2 = step(i, carry, True)
    o_ref[...] = _diff_combine(acc1_ref[...] / l1, acc2_ref[...] / l2, lam_ref, sg_ref, lambda_init)


def _flash_fast_body(lam_ref, sg_ref, q_ref, k_ref, vt_ref, o_ref, acc_ref, p_ref,
                     *, tq, heads, lambda_init):
    i = pl.program_id(2)
    lanes = lambda h: slice(h * LANES, (h + 1) * LANES)
    q2 = [jnp.concatenate(_split_components(q_ref[:, lanes(h)]), axis=0) for h in range(heads)]
    ones_rows = jnp.ones((ONES_ROWS, tq), BF16)
    key = lax.broadcasted_iota(jnp.int32, (tq, 2 * tq), 0)
    qry = lax.broadcasted_iota(jnp.int32, (tq, 2 * tq), 1)
    delta = key - jnp.where(qry >= tq, qry - tq, qry)

    def probs_t(h, j):
        p = jnp.exp2(_scores(k_ref[pl.ds(j * tq, tq), lanes(h)], q2[h]))
        return jnp.where(delta <= (i - j) * tq, p, 0.0).astype(BF16)

    def weighted_values_t(h, j):
        start = pl.multiple_of(j * tq, tq)
        vt = jnp.concatenate([vt_ref[h, :, pl.ds(start, tq)], ones_rows], axis=0)
        return _dot(vt, p_ref[h])

    for h in range(heads):
        p_ref[h] = probs_t(h, 0)
    acc_ref[...] = jnp.zeros_like(acc_ref)

    def body(j, carry):
        for h in range(heads):
            acc_ref[h] += weighted_values_t(h, j - 1)
            p_ref[h] = probs_t(h, j)
        return carry

    lax.fori_loop(1, i + 1, body, 0)
    lam = lam_ref[...]
    dot_sum = lambda a, b: jnp.sum(lam[a:a + 1] * lam[b:b + 1], axis=-1, keepdims=True)
    lam_full = jnp.exp(dot_sum(0, 1)) - jnp.exp(dot_sum(2, 3)) + lambda_init
    for h in range(heads):
        acc = acc_ref[h] + weighted_values_t(h, i)
        o = acc[:LANES] / acc[LANES:LANES + 1]
        o = o[:, :tq] - lam_full * o[:, tq:]
        o = o * lax.rsqrt(jnp.mean(o * o, axis=0, keepdims=True) + EPS) * sg_ref[...] * (1.0 - lambda_init)
        o_ref[:, lanes(h)] = o.T.astype(BF16)


def _flash_call(body, name, scratch, heads_per_step, transposed_v, q, k, v, lam, sub_gain,
                lambda_init, tq):
    b, l, d = q.shape
    width = heads_per_step * LANES
    qspec = pl.BlockSpec((None, tq, width), lambda bi, hi, i: (bi, i, hi))
    kspec = pl.BlockSpec((None, l, width), lambda bi, hi, i: (bi, 0, hi))
    vspec = kspec
    if transposed_v:
        vspec = pl.BlockSpec((None, heads_per_step, LANES, l), lambda bi, hi, i: (bi, hi, 0, 0))
    return pl.pallas_call(
        functools.partial(body, tq=tq, lambda_init=lambda_init),
        out_shape=jax.ShapeDtypeStruct((b, l, d), BF16),
        grid=(b, d // width, l // tq),
        in_specs=[_resident(lam.shape), _resident(sub_gain.shape), qspec, kspec, vspec],
        out_specs=qspec,
        scratch_shapes=scratch,
        compiler_params=_params("parallel", "parallel", "arbitrary"),
        name=name,
    )(lam, sub_gain, q, k, v)


def _flash_attention(q, k, v, vt, lam, sub_gain, score_bound, lambda_init, tq_fast=512, tq_safe=256, hps=4):
    tqf = min(tq_fast, q.shape[1])
    tqs = min(tq_safe, q.shape[1])
    sg = sub_gain.astype(F32)
    args = (q, k, v, vt, lam.astype(F32), sg)
    fast_scratch = [pltpu.VMEM((hps, LANES + ONES_ROWS, 2 * tqf), F32), pltpu.VMEM((hps, tqf, 2 * tqf), BF16)]
    safe_scratch = [pltpu.VMEM((tqs, LANES), F32)] * 2
    fast = lambda q, k, v, vt, lam, sg: _flash_call(
        functools.partial(_flash_fast_body, heads=hps), "flash_fast", fast_scratch, hps, True,
        q, k, vt, lam, sg[:, None], lambda_init, tqf)
    safe = lambda q, k, v, vt, lam, sg: _flash_call(
        _flash_safe_body, "flash_safe", safe_scratch, 1, False, q, k, v, lam, sg[None, :], lambda_init, tqs)
    return lax.cond(score_bound <= FAST_SCORE_BOUND, fast, safe, *args)


def _norm_proj_body(x_ref, g_ref, w_ref, o_ref):
    o_ref[...] = _dot(_rms(x_ref[...], g_ref[...]).astype(BF16), w_ref[...])


def _norm_proj(x, gain, w, tm=512):
    n, d = x.shape
    tm = min(tm, n)
    return pl.pallas_call(
        _norm_proj_body,
        out_shape=jax.ShapeDtypeStruct((n, w.shape[1]), F32),
        grid=(n // tm,),
        in_specs=[pl.BlockSpec((tm, d), lambda i: (i, 0)), _resident((1, d)), w.spec()],
        out_specs=pl.BlockSpec((tm, w.shape[1]), lambda i: (i, 0)),
        compiler_params=_params("parallel"),
        name="norm_proj",
    )(x, gain[None, :], w.stack)


def _relu2_mlp(x, g_ref, w1_ref, w2_ref, th):
    h = _rms(x, g_ref[...]).astype(BF16)
    acc = x
    for j in range(w1_ref.shape[1] // th):
        a = _dot(h, w1_ref[:, j * th:(j + 1) * th])
        a = jnp.square(jnp.maximum(a, 0.0)).astype(BF16)
        acc = acc + _dot(a, w2_ref[j * th:(j + 1) * th, :])
    return acc


def _mlp_body(x_ref, g_ref, w1_ref, w2_ref, o_ref, *, th):
    o_ref[...] = _relu2_mlp(x_ref[...], g_ref, w1_ref, w2_ref, th)


def _proj_mlp_body(a_ref, wp_ref, x_ref, g_ref, w1_ref, w2_ref, o_ref, *, th):
    x = x_ref[...] + _dot(a_ref[...], wp_ref[...])
    o_ref[...] = _relu2_mlp(x, g_ref, w1_ref, w2_ref, th)


def _glu_mlp_body(a_ref, wg_ref, bg_ref, wp_ref, x_ref, g_ref, w1_ref, w2_ref, o_ref, *, th):
    a = a_ref[...]
    gate = jax.nn.sigmoid(_dot(a, wg_ref[...]) + bg_ref[...])
    x = x_ref[...] + _dot((a.astype(F32) * gate).astype(BF16), wp_ref[...])
    o_ref[...] = _relu2_mlp(x, g_ref, w1_ref, w2_ref, th)


def _mlp(x, gain, w1, w2, proj=None, glu=None, tm=512, th=1024):
    n, d = x.shape
    tm = min(tm, n)
    row = pl.BlockSpec((tm, d), lambda i: (i, 0))
    pre, pre_specs, body, name = (), [], _mlp_body, "relu2_mlp"
    if proj is not None:
        a, wp = proj
        pre, body, name = (a, wp.stack), _proj_mlp_body, "proj_relu2_mlp"
        pre_specs = [pl.BlockSpec((tm, a.shape[1]), lambda i: (i, 0)), wp.spec()]
    elif glu is not None:
        a, wg, bg, wp = glu
        pre, body, name = (a, wg.stack, bg.astype(F32)[None, :], wp.stack), _glu_mlp_body, "glu_relu2_mlp"
        pre_specs = [row, wg.spec(), _resident((1, d)), wp.spec()]
    return pl.pallas_call(
        functools.partial(body, th=th),
        out_shape=jax.ShapeDtypeStruct((n, d), F32),
        grid=(n // tm,),
        in_specs=pre_specs + [row, _resident((1, d)), w1.spec(), w2.spec()],
        out_specs=row,
        compiler_params=_params("parallel"),
        name=name,
    )(*pre, x, gain[None, :], w1.stack, w2.stack)


def _s5_build_operators(bd_ref, bb_ref, cb_ref, pw_ref, pwc_ref, m_ref, ws_ref, wo_ref):
    t = S5_CHUNK
    half = pw_ref.shape[2]
    m_ref[...] = jnp.zeros_like(m_ref)
    for s in range(t):
        for tt in range(s, t):
            m_ref[s * LANES:(s + 1) * LANES, tt * LANES:(tt + 1) * LANES] = bd_ref[tt - s]
    bbr, bbi = bb_ref[0], bb_ref[1]
    for s in range(t):
        pr = pw_ref[0, t - 1 - s:t - s, :]
        pi = pw_ref[1, t - 1 - s:t - s, :]
        ws_ref[s * LANES:(s + 1) * LANES, 0:half] = (bbr * pr - bbi * pi).astype(BF16)
        ws_ref[s * LANES:(s + 1) * LANES, half:2 * half] = (bbr * pi + bbi * pr).astype(BF16)
    cbr, cbi = cb_ref[0], cb_ref[1]
    for tt in range(t):
        pr = pwc_ref[0, :, tt + 1:tt + 2]
        pi = pwc_ref[1, :, tt + 1:tt + 2]
        wo_ref[0:half, tt * LANES:(tt + 1) * LANES] = (cbr * pr - cbi * pi).astype(BF16)
        wo_ref[half:2 * half, tt * LANES:(tt + 1) * LANES] = (-(cbr * pi + cbi * pr)).astype(BF16)


def _s5_scan_body(u_ref, bd_ref, bb_ref, cb_ref, pw_ref, pwc_ref, d_ref, o_ref,
                  m_ref, ws_ref, wo_ref, u2_ref, s_ref, xp_ref, y_ref, *, n_chunks):
    t = S5_CHUNK
    half = pw_ref.shape[2]

    @pl.when(pl.program_id(1) == 0)
    def _():
        _s5_build_operators(bd_ref, bb_ref, cb_ref, pw_ref, pwc_ref, m_ref, ws_ref, wo_ref)

    for s in range(t):
        u2_ref[:, s * LANES:(s + 1) * LANES] = u_ref[pl.ds(s, n_chunks, stride=t), :].astype(BF16)
    s_ref[...] = _dot(u2_ref[...], ws_ref[...])
    ar = pw_ref[0, t:t + 1, :]
    ai = pw_ref[1, t:t + 1, :]

    def carry_step(c, carry):
        xr, xi = carry
        xp_ref[pl.ds(c, 1), 0:half] = xr
        xp_ref[pl.ds(c, 1), half:2 * half] = xi
        sr = s_ref[pl.ds(c, 1), 0:half]
        si = s_ref[pl.ds(c, 1), half:2 * half]
        return ar * xr - ai * xi + sr, ar * xi + ai * xr + si

    zero = jnp.zeros((1, half), F32)
    lax.fori_loop(0, n_chunks, carry_step, (zero, zero))
    xp = xp_ref[...].astype(BF16)
    per = MXU_DIM // LANES
    for cb in range(t // per):
        cols = slice(cb * MXU_DIM, (cb + 1) * MXU_DIM)
        kdim = (cb + 1) * MXU_DIM
        y2 = _dot(u2_ref[:, 0:kdim], m_ref[0:kdim, cols]) + _dot(xp, wo_ref[:, cols])
        for r in range(per):
            y_ref[pl.ds(cb * per + r, n_chunks, stride=t), :] = y2[:, r * LANES:(r + 1) * LANES]
    y = y_ref[...] + d_ref[...] * u_ref[...]
    o_ref[...] = _gelu(y).astype(BF16)


def _s5_weights(a_re, a_im, b_re, b_im, c_re, c_im, log_dt):
    t = S5_CHUNK
    g, p = a_re.shape
    gl = LANES // S5_GROUP
    nj = g // gl
    hp = lax.Precision.HIGHEST
    lr = jnp.minimum(a_re.astype(F32), -1e-4)
    li = a_im.astype(F32)
    dt = jnp.exp(log_dt.astype(F32))[:, None]
    steps = jnp.arange(t + 1, dtype=F32)[:, None, None]
    mag = jnp.exp(steps * (lr * dt))
    pr = mag * jnp.cos(steps * (li * dt))
    pi = mag * jnp.sin(steps * (li * dt))
    nr, ni = pr[1] - 1.0, pi[1]
    den = lr * lr + li * li
    zr = ((nr * lr + ni * li) / den)[..., None]
    zi = ((ni * lr - nr * li) / den)[..., None]
    br, bi = b_re.astype(F32), b_im.astype(F32)
    bbr = zr * br - zi * bi
    bbi = zr * bi + zi * br
    cr, ci = c_re.astype(F32), c_im.astype(F32)
    wr = pr[:t, :, :, None] * bbr[None] - pi[:t, :, :, None] * bbi[None]
    wi = pr[:t, :, :, None] * bbi[None] + pi[:t, :, :, None] * bbr[None]
    kern = (jnp.einsum('ghp,kgpi->kgih', cr, wr, precision=hp)
            - jnp.einsum('ghp,kgpi->kgih', ci, wi, precision=hp))
    def group_diagonal(w, rows_per_group, cols_per_group):
        w = jnp.tile(w, (1,) * (w.ndim - 1) + (gl,))
        rg = jnp.arange(w.shape[-2])[:, None] // rows_per_group
        cg = jnp.arange(w.shape[-1])[None, :] // cols_per_group
        return jnp.where(rg == cg, w, 0.0)

    kern = kern.reshape(t, nj, gl * S5_GROUP, S5_GROUP)
    bd = group_diagonal(kern, S5_GROUP, S5_GROUP).astype(BF16)
    bb = jnp.stack([bbr, bbi]).transpose(0, 1, 3, 2).reshape(2, nj, gl * S5_GROUP, p)
    bblk = group_diagonal(bb, S5_GROUP, p)
    cc = jnp.stack([cr, ci]).transpose(0, 1, 3, 2).reshape(2, nj, gl * p, S5_GROUP)
    cblk = group_diagonal(cc, p, S5_GROUP)
    pw = jnp.stack([pr, pi]).reshape(2, t + 1, nj, gl * p)
    return bd, bblk, cblk, pw.transpose(2, 0, 1, 3), pw.transpose(0, 2, 3, 1)


_S5_BLOCK_AXES = (1, 1, 1, 0, 1)


def _s5_scan(u, weights, d_skip):
    b, l, d = u.shape
    nj = weights[0].shape[_S5_BLOCK_AXES[0]]
    n_chunks = l // S5_CHUNK
    tl = S5_CHUNK * LANES
    ns = 2 * weights[3].shape[3]
    col = pl.BlockSpec((None, l, LANES), lambda j, bi: (bi, 0, j))

    def per_j(w, axis):
        block = tuple(None if a == axis else s for a, s in enumerate(w.shape))
        return pl.BlockSpec(block, lambda j, bi: tuple(j if a == axis else 0 for a in range(w.ndim)))

    return pl.pallas_call(
        functools.partial(_s5_scan_body, n_chunks=n_chunks),
        out_shape=jax.ShapeDtypeStruct((b, l, d), BF16),
        grid=(nj, b),
        in_specs=[col] + [per_j(w, a) for w, a in zip(weights, _S5_BLOCK_AXES)]
        + [pl.BlockSpec((1, LANES), lambda j, bi: (0, j))],
        out_specs=col,
        scratch_shapes=[pltpu.VMEM((tl, tl), BF16), pltpu.VMEM((tl, ns), BF16), pltpu.VMEM((ns, tl), BF16),
                        pltpu.VMEM((n_chunks, tl), BF16), pltpu.VMEM((n_chunks, ns), F32),
                        pltpu.VMEM((n_chunks, ns), F32), pltpu.VMEM((l, LANES), F32)],
        compiler_params=_params("arbitrary", "arbitrary"),
        name="s5_scan",
    )(u, *weights, d_skip.astype(F32)[None, :])


def _gmlp_body(x_ref, g_ref, win_ref, vg_ref, wsp_ref, bs_ref, wout_ref, o_ref, v_ref,
               *, half, heads):
    x = x_ref[...]
    tm = x.shape[0]
    hd = half // heads
    h = _rms(x, g_ref[...]).astype(BF16)
    per = 2
    wide = per * hd
    ss = jnp.zeros((tm, 1), F32)
    for c in range(heads // per):
        zv = _gelu(_dot(h, win_ref[:, half + c * wide:half + (c + 1) * wide]))
        ss = ss + jnp.sum(zv * zv, axis=-1, keepdims=True)
        v_ref[:, c * wide:(c + 1) * wide] = zv
    rinv = lax.rsqrt(ss * (1.0 / half) + EPS)
    row = lax.broadcasted_iota(jnp.int32, (GM_CHUNK, GM_CHUNK), 0)
    col = lax.broadcasted_iota(jnp.int32, (GM_CHUNK, GM_CHUNK), 1)

    def head_gate(head):
        sl = slice(head * hd, (head + 1) * hd)
        vh = (v_ref[:, sl] * rinv * vg_ref[:, sl]).astype(BF16)
        wm = jnp.where(col <= row, wsp_ref[head], 0.0).astype(BF16)
        bias = bs_ref[:, head:head + 1]
        return jnp.concatenate(
            [_dot(wm, vh[r * GM_CHUNK:(r + 1) * GM_CHUNK, :]) + bias for r in range(tm // GM_CHUNK)],
            axis=0)

    acc = x
    for c in range(heads // per):
        sl = slice(c * wide, (c + 1) * wide)
        zu = _gelu(_dot(h, win_ref[:, sl]))
        gate = jnp.concatenate([head_gate(c * per + r) for r in range(per)], axis=1)
        acc = acc + _dot((zu * gate).astype(BF16), wout_ref[sl, :])
    o_ref[...] = acc


def _gmlp(x, gain, w_in, v_gain, w_s, b_s, w_out, tm=256):
    n, d = x.shape
    tm = min(tm, n)
    half = w_out.shape[0]
    heads = w_s.shape[0]
    row = pl.BlockSpec((tm, d), lambda i: (i, 0))
    return pl.pallas_call(
        functools.partial(_gmlp_body, half=half, heads=heads),
        out_shape=jax.ShapeDtypeStruct((n, d), F32),
        grid=(n // tm,),
        in_specs=[row, _resident((1, d)), w_in.spec(), _resident((1, half)),
                  _resident(w_s.shape), _resident((GM_CHUNK, heads)), w_out.spec()],
        out_specs=row,
        scratch_shapes=[pltpu.VMEM((tm, half), F32)],
        compiler_params=_params("parallel"),
        name="gmlp",
    )(x, gain[None, :], w_in.stack, v_gain.astype(F32)[None, :], w_s.astype(F32), b_s.astype(F32).T,
      w_out.stack)


def kernel(x, positions, norm_mix, norm_mlp, mlp_w1, mlp_w2, attn_w_qkv, attn_q_norm, attn_k_norm, attn_lambda, attn_sub_norm, attn_w_o, ssm_w_in, ssm_a_re, ssm_a_im, ssm_b_re, ssm_b_im, ssm_c_re, ssm_c_im, ssm_d, ssm_log_dt, ssm_w_glu, ssm_b_glu, ssm_w_out, gm_w_in, gm_v_norm, gm_w_s, gm_b_s, gm_w_out):
    b, l, d = x.shape
    n = b * l
    depth = norm_mix.shape[0]
    xf = x.reshape(n, d)
    cos, sin = _rope_tables(positions)
    for i in range(depth):
        kind = i % N_MIXERS
        j = i // N_MIXERS
        mlp = functools.partial(_mlp, xf, norm_mlp[i], _Layer(mlp_w1, i), _Layer(mlp_w2, i))
        if kind == 0:
            lambda_init = 0.8 - 0.6 * math.exp(-0.3 * i)
            q, k, v, vt = _qkv_proj(xf, norm_mix[i], _Layer(attn_w_qkv, j), attn_q_norm[j], attn_k_norm[j],
                                    cos, sin, b)
            shape = (b, l, d)
            score_bound = (DA_HEAD_DIM ** 0.5 * jnp.max(jnp.abs(attn_q_norm[j]))
                           * jnp.max(jnp.abs(attn_k_norm[j])))
            o = _flash_attention(q.reshape(shape), k.reshape(shape), v.reshape(shape), vt,
                                 attn_lambda[j], attn_sub_norm[j], score_bound, lambda_init)
            xf = mlp(proj=(o.reshape(n, d), _Layer(attn_w_o, j)))
        elif kind == 1:
            u = _norm_proj(xf, norm_mix[i], _Layer(ssm_w_in, j))
            weights = _s5_weights(ssm_a_re[j], ssm_a_im[j], ssm_b_re[j], ssm_b_im[j],
                                  ssm_c_re[j], ssm_c_im[j], ssm_log_dt[j])
            g = _s5_scan(u.reshape(b, l, d), weights, ssm_d[j])
            xf = mlp(glu=(g.reshape(n, d), _Layer(ssm_w_glu, j), ssm_b_glu[j], _Layer(ssm_w_out, j)))
        else:
            xf = _gmlp(xf, norm_mix[i], _Layer(gm_w_in, j), gm_v_norm[j], gm_w_s[j], gm_b_s[j],
                       _Layer(gm_w_out, j))
            xf = _mlp(xf, norm_mlp[i], _Layer(mlp_w1, i), _Layer(mlp_w2, i))
    return xf.reshape(b, l, d)
```
